```python
import jax, jax.numpy as jnp
from jax import lax
import numpy as np

D_MODEL = 1024
BATCH = 4
SEQ = 4096
DEPTH = 1
DEC_BATCH = 32
DEC_SEQ = 16
PAST_LEN = 2048

CHUNK = 64
D_A = 512
A_GROUPS = 4
A_GROUP_DIM = D_A // A_GROUPS
GMLP_CHUNK = 128
D_B = 512
B_HEADS = 8
B_HEAD_DIM = D_B // B_HEADS
CONV_WIDTH = 4
LRU_C = 8.0
D_IN = 2 * D_A + 2 * D_B
D_MIX = D_A + D_B
PEER_HEADS = 8
N_KEYS = 128
N_EXPERTS = N_KEYS * N_KEYS
D_KEY = 256
D_HALF = D_KEY // 2
PEER_TOPK = 16
PEER_BLOCK = 128
EPS = 1e-6

kernel_name = "hybrid_gmlp_rglru_peer_stream_step"


def rmsnorm(x, g):
    xf = x.astype(jnp.float32)
    y = xf * lax.rsqrt(jnp.mean(xf * xf, axis=-1, keepdims=True) + EPS) * g.astype(jnp.float32)
    return y.astype(x.dtype)


def gmlp_mix(z, g_v, w_s, b_s):
    u, v = jnp.split(z, 2, axis=-1)
    v = rmsnorm(v, g_v)
    bsz, t, _ = v.shape
    L = min(t, GMLP_CHUNK)
    n = t // L
    vc = v.reshape(bsz, n, L, A_GROUPS, A_GROUP_DIM)
    mask = jnp.tril(jnp.ones((L, L), dtype=w_s.dtype))
    w = w_s[:, :L, :L] * mask
    s = jnp.einsum("gnm,bcmgd->bcngd", w, vc) + jnp.transpose(b_s[:, :L])[None, None, :, :, None]
    return u * s.reshape(bsz, t, D_A), v


def causal_conv(xb, buf, w, b):
    t = xb.shape[1]
    xp = jnp.concatenate([buf.astype(xb.dtype), xb], axis=1)
    y = b + xp[:, 0:t] * w[0]
    for k in range(1, CONV_WIDTH):
        y = y + xp[:, k:k + t] * w[k]
    return y, xp[:, -(CONV_WIDTH - 1):]


def _lin_combine(left, right):
    a1, b1 = left
    a2, b2 = right
    return a1 * a2, a2 * b1 + b2


def rg_lru(xc, h0, w_a, b_a, w_i, b_i, lam):
    bsz, t, _ = xc.shape
    xh = xc.reshape(bsz, t, B_HEADS, B_HEAD_DIM)
    r = jax.nn.sigmoid(jnp.einsum("bthi,hij->bthj", xh, w_a).reshape(bsz, t, D_B) + b_a)
    i = jax.nn.sigmoid(jnp.einsum("bthi,hij->bthj", xh, w_i).reshape(bsz, t, D_B) + b_i)
    log_a = LRU_C * r.astype(jnp.float32) * jax.nn.log_sigmoid(lam.astype(jnp.float32))
    a = jnp.exp(log_a)
    bx = jnp.sqrt(-jnp.expm1(2.0 * log_a)) * (i * xc).astype(jnp.float32)
    a_cum, b_cum = lax.associative_scan(_lin_combine, (a, bx), axis=1)
    h = a_cum * h0.astype(jnp.float32)[:, None, :] + b_cum
    return h.astype(xc.dtype), h[:, -1].astype(h0.dtype)


def peer(h, w_q, k1, k2, u_tab, v_tab):
    bsz, t, d = h.shape
    n = bsz * t
    nb = -(-n // PEER_BLOCK)
    xt = jnp.pad(h.reshape(n, d), ((0, nb * PEER_BLOCK - n), (0, 0))).reshape(nb, PEER_BLOCK, d)

    def block(xb):
        q = (xb @ w_q).reshape(PEER_BLOCK, PEER_HEADS, 2, D_HALF)
        s1 = jnp.einsum("thd,kd->thk", q[:, :, 0], k1).astype(jnp.float32)
        s2 = jnp.einsum("thd,kd->thk", q[:, :, 1], k2).astype(jnp.float32)
        v1, i1 = lax.top_k(s1, PEER_TOPK)
        v2, i2 = lax.top_k(s2, PEER_TOPK)
        cand = (v1[..., :, None] + v2[..., None, :]).reshape(PEER_BLOCK, PEER_HEADS, PEER_TOPK * PEER_TOPK)
        cidx = (i1[..., :, None] * N_KEYS + i2[..., None, :]).reshape(PEER_BLOCK, PEER_HEADS, PEER_TOPK * PEER_TOPK)
        top_s, sel = lax.top_k(cand, PEER_TOPK)
        eidx = jnp.take_along_axis(cidx, sel, axis=-1)
        g = jax.nn.softmax(top_s, axis=-1)
        u = u_tab[eidx]
        act = jax.nn.gelu(jnp.einsum("thkd,td->thk", u, xb).astype(jnp.float32))
        coef = (g * act).astype(xb.dtype)
        return jnp.einsum("thk,thkd->td", coef, v_tab[eidx])

    out = lax.map(block, xt).reshape(nb * PEER_BLOCK, d)[:n]
    return out.reshape(bsz, t, d)


def layer(x, conv_buf, h0, g_mix, w_in, g_v, w_s, b_s, conv_w, conv_b, w_a, b_a, w_i, b_i, lam,
          g_out_a, g_out_b, w_out, g_ffn, w_q, k1, k2, u_tab, v_tab):
    h = rmsnorm(x, g_mix)
    z = h @ w_in
    za = z[..., :2 * D_A]
    xb = z[..., 2 * D_A:2 * D_A + D_B]
    gate = z[..., 2 * D_A + D_B:]
    ya, v_rows = gmlp_mix(jax.nn.gelu(za), g_v, w_s, b_s)
    xc, new_buf = causal_conv(xb, conv_buf, conv_w, conv_b)
    yb, h_last = rg_lru(xc, h0, w_a, b_a, w_i, b_i, lam)
    yb = yb * jax.nn.gelu(gate)
    mix = jnp.concatenate([rmsnorm(ya, g_out_a), rmsnorm(yb, g_out_b)], axis=-1) @ w_out
    x = x + mix
    x = x + peer(rmsnorm(x, g_ffn), w_q, k1, k2, u_tab, v_tab)
    return x, new_buf, h_last, v_rows


def setup_inputs(seed: int = 0) -> dict:
    key = jax.random.key(seed)
    ks = jax.random.split(key, 32)
    f32 = jnp.float32
    nrm = lambda k, shape, s: jax.random.normal(k, shape, f32) * s
    gain = lambda k, shape: 1.0 + 0.01 * jax.random.normal(k, shape, f32)
    a0 = jax.random.uniform(ks[13], (DEPTH, D_B), f32, 0.9, 0.999) ** (1.0 / LRU_C)
    lam = jnp.log(a0) - jnp.log1p(-a0)
    return {
        "x_prompt": jax.random.normal(ks[0], (BATCH, SEQ, D_MODEL), f32),
        "x_sample": jax.random.normal(ks[1], (DEC_BATCH, DEC_SEQ, D_MODEL), f32),
        "state_conv": jax.random.normal(ks[2], (DEPTH, DEC_BATCH, CONV_WIDTH - 1, D_B), f32),
        "state_lru": 0.5 * jax.random.normal(ks[3], (DEPTH, DEC_BATCH, D_B), f32),
        "g_mix": gain(ks[4], (DEPTH, D_MODEL)),
        "w_in": nrm(ks[5], (DEPTH, D_MODEL, D_IN), D_MODEL ** -0.5),
        "g_v": gain(ks[6], (DEPTH, D_A)),
        "w_s": nrm(ks[7], (DEPTH, A_GROUPS, GMLP_CHUNK, GMLP_CHUNK), GMLP_CHUNK ** -0.5),
        "b_s": 1.0 + 0.1 * jax.random.normal(ks[8], (DEPTH, A_GROUPS, GMLP_CHUNK), f32),
        "conv_w": nrm(ks[9], (DEPTH, CONV_WIDTH, D_B), CONV_WIDTH ** -0.5),
        "conv_b": nrm(ks[10], (DEPTH, D_B), 0.02),
        "w_a": nrm(ks[11], (DEPTH, B_HEADS, B_HEAD_DIM, B_HEAD_DIM), B_HEAD_DIM ** -0.5),
        "b_a": nrm(ks[12], (DEPTH, D_B), 0.02),
        "w_i": nrm(ks[14], (DEPTH, B_HEADS, B_HEAD_DIM, B_HEAD_DIM), B_HEAD_DIM ** -0.5),
        "b_i": nrm(ks[15], (DEPTH, D_B), 0.02),
        "lam": lam,
        "g_out_a": gain(ks[16], (DEPTH, D_A)),
        "g_out_b": gain(ks[17], (DEPTH, D_B)),
        "w_out": nrm(ks[18], (DEPTH, D_MIX, D_MODEL), D_MIX ** -0.5),
        "g_ffn": gain(ks[19], (DEPTH, D_MODEL)),
        "w_q": nrm(ks[20], (DEPTH, D_MODEL, PEER_HEADS * D_KEY), D_MODEL ** -0.5),
        "k_sub1": nrm(ks[21], (DEPTH, N_KEYS, D_HALF), D_HALF ** -0.5),
        "k_sub2": nrm(ks[22], (DEPTH, N_KEYS, D_HALF), D_HALF ** -0.5),
        "u_tab": nrm(ks[23], (DEPTH, N_EXPERTS, D_MODEL), D_MODEL ** -0.5),
        "v_tab": nrm(ks[24], (DEPTH, N_EXPERTS, D_MODEL), PEER_HEADS ** -0.5),
        "g_final": gain(ks[25], (D_MODEL,)),
    }


def reference(x_prompt, x_sample, state_conv, state_lru, g_mix, w_in, g_v, w_s, b_s, conv_w, conv_b,
              w_a, b_a, w_i, b_i, lam, g_out_a, g_out_b, w_out, g_ffn, w_q, k_sub1, k_sub2,
              u_tab, v_tab, g_final):
    xp = x_prompt
    xs = x_sample
    conv_p, lru_p, conv_s, lru_s, vrows_s = [], [], [], [], []
    for l in range(DEPTH):
        p = (g_mix[l], w_in[l], g_v[l], w_s[l], b_s[l], conv_w[l], conv_b[l], w_a[l], b_a[l],
             w_i[l], b_i[l], lam[l], g_out_a[l], g_out_b[l], w_out[l], g_ffn[l], w_q[l],
             k_sub1[l], k_sub2[l], u_tab[l], v_tab[l])
        buf0 = jnp.zeros((xp.shape[0], CONV_WIDTH - 1, D_B), xp.dtype)
        h00 = jnp.zeros((xp.shape[0], D_B), xp.dtype)
        xp, cb, hl, _ = layer(xp, buf0, h00, *p)
        conv_p.append(cb)
        lru_p.append(hl)
        xs, cb, hl, vr = layer(xs, state_conv[l], state_lru[l], *p)
        conv_s.append(cb)
        lru_s.append(hl)
        vrows_s.append(vr)
    y_prompt = rmsnorm(xp, g_final)
    y_sample = rmsnorm(xs, g_final)
    new_conv_prompt = jnp.stack(conv_p)
    new_lru_prompt = jnp.stack(lru_p)
    new_conv_sample = jnp.stack(conv_s)
    new_lru_sample = jnp.stack(lru_s)
    new_gmlp_v_sample = jnp.stack(vrows_s)
    return (y_prompt, y_sample, new_conv_prompt, new_lru_prompt, new_conv_sample, new_lru_sample, new_gmlp_v_sample)
```

```python
import functools

import jax
import jax.numpy as jnp
from jax import lax
from jax.experimental import pallas as pl
from jax.experimental.pallas import tpu as pltpu

EPS = 1e-6
LRU_C = 8.0
CONV_WIDTH = 4
A_GROUPS = 4
PEER_HEADS = 8
PEER_TOPK = 16
GMLP_CHUNK = 128

LANES = 128
SUBLANES = 8
PACKED_ROWS = 16
VMEM_LIMIT_BYTES = 56 * 1024 * 1024

MXU_DTYPE = jnp.bfloat16
GATE_DTYPE = jnp.bfloat16

RANK_BASE = 2.0 ** 100
RANK_STEP = 2.0 ** 80

_NT = (((1,), (1,)), ((), ()))


def _rms(x, g):
    return x * lax.rsqrt(jnp.mean(x * x, axis=-1, keepdims=True) + EPS) * g


def _mm(a, b):
    return jnp.dot(a.astype(MXU_DTYPE), b.astype(MXU_DTYPE), preferred_element_type=jnp.float32)


def _mm_nt(a, b):
    return lax.dot_general(a.astype(MXU_DTYPE), b.astype(MXU_DTYPE), _NT, preferred_element_type=jnp.float32)


def _seg_scan(a, b, seg):
    rows = lax.broadcasted_iota(jnp.int32, a.shape, 0)
    pos = rows & (seg - 1)
    s = 1
    while s < seg:
        a_sh = pltpu.roll(a, s, 0)
        b_sh = pltpu.roll(b, s, 0)
        valid = pos >= s
        b = jnp.where(valid, a * b_sh + b, b)
        a = jnp.where(valid, a * a_sh, a)
        s *= 2
    return a, b


def _mixer_front(x, g_mix, w_in, g_v, d_a, d_b):
    h = _rms(x, g_mix)
    z = _mm(h, w_in)
    za = jax.nn.gelu(z[:, :2 * d_a])
    u = za[:, :d_a]
    v = _rms(za[:, d_a:], g_v)
    xb = z[:, 2 * d_a:2 * d_a + d_b]
    gate = z[:, 2 * d_a + d_b:]
    return u, v, xb, gate


def _gmlp(u, v, wmix_ref, bias, chunk_len):
    rows, d_a = v.shape
    gd = d_a // A_GROUPS
    r = lax.broadcasted_iota(jnp.int32, (GMLP_CHUNK, GMLP_CHUNK), 0)
    c = lax.broadcasted_iota(jnp.int32, (GMLP_CHUNK, GMLP_CHUNK), 1)
    mask = (r >= c) & ((r // chunk_len) == (c // chunk_len))
    vb = v.astype(MXU_DTYPE)
    blocks = []
    for blk in range(rows // GMLP_CHUNK):
        cols = []
        for g in range(A_GROUPS):
            w = jnp.where(mask, wmix_ref[g], 0.0).astype(MXU_DTYPE)
            vg = vb[blk * GMLP_CHUNK:(blk + 1) * GMLP_CHUNK, g * gd:(g + 1) * gd]
            cols.append(jnp.dot(w, vg, preferred_element_type=jnp.float32))
        blocks.append(jnp.concatenate(cols, axis=1) + bias)
    s = blocks[0] if len(blocks) == 1 else jnp.concatenate(blocks, axis=0)
    return u * s


def _lru_inputs(xc, wa, ba, wi, bi, lam):
    r = jax.nn.sigmoid(_mm(xc, wa) + ba)
    i = jax.nn.sigmoid(_mm(xc, wi) + bi)
    log_sig = jnp.minimum(lam, 0.0) - jnp.log1p(jnp.exp(-jnp.abs(lam)))
    log_a = LRU_C * r * log_sig
    a = jnp.exp(log_a)
    t = jnp.tanh(log_a)
    bx = jnp.sqrt(-2.0 * t / (1.0 - t)) * (i * xc)
    return a, bx


def _mixer_back(x, ya, h, gate, g_out_a, g_out_b, w_out_ref, d_a):
    yb = h * jax.nn.gelu(gate)
    mix = _mm(_rms(ya, g_out_a), w_out_ref[:d_a, :]) + _mm(_rms(yb, g_out_b), w_out_ref[d_a:, :])
    return x + mix


def _prompt_mixer_kernel(x_ref, g_mix_ref, w_in_ref, g_v_ref, wmix_ref, bias_ref, cw_ref, cb_ref, wa_ref, ba_ref,
                         wi_ref, bi_ref, lam_ref, goa_ref, gob_ref, w_out_ref,
                         x1_ref, xb_tail_ref, h_tail_ref, conv_carry, h_carry, *, d_a, d_b):
    t = pl.program_id(1)

    @pl.when(t == 0)
    def _():
        conv_carry[...] = jnp.zeros_like(conv_carry)
        h_carry[...] = jnp.zeros_like(h_carry)

    x = x_ref[...]
    rows = x.shape[0]
    u, v, xb, gate = _mixer_front(x, g_mix_ref[...], w_in_ref[...], g_v_ref[...], d_a, d_b)
    ya = _gmlp(u, v, wmix_ref, bias_ref[...], GMLP_CHUNK)

    cw = cw_ref[...]
    xc = cb_ref[...] + xb * cw[CONV_WIDTH - 1:CONV_WIDTH, :]
    prev = conv_carry[...]
    row8 = lax.broadcasted_iota(jnp.int32, (SUBLANES, d_b), 0)
    for k in range(1, CONV_WIDTH):
        rolled = pltpu.roll(xb, k, 0)
        head = jnp.where(row8 < k, pltpu.roll(prev, k, 0), rolled[:SUBLANES, :])
        shifted = jnp.concatenate([head, rolled[SUBLANES:, :]], axis=0)
        xc = xc + shifted * cw[CONV_WIDTH - 1 - k:CONV_WIDTH - k, :]
    conv_carry[...] = xb[rows - SUBLANES:, :]

    a, bx = _lru_inputs(xc, wa_ref[...], ba_ref[...], wi_ref[...], bi_ref[...], lam_ref[...])
    a_cum, b_cum = _seg_scan(a, bx, rows)
    h = a_cum * h_carry[...] + b_cum
    h_carry[...] = h[rows - 1:rows, :]

    x1_ref[...] = _mixer_back(x, ya, h, gate, goa_ref[...], gob_ref[...], w_out_ref, d_a)
    xb_tail_ref[0] = xb[rows - SUBLANES:, :]
    h_tail_ref[0] = h[rows - SUBLANES:, :]


def _sample_mixer_kernel(x_ref, cs_ref, h0_ref, g_mix_ref, w_in_ref, g_v_ref, wmix_ref, bias_ref, cw_ref, cb_ref,
                         wa_ref, ba_ref, wi_ref, bi_ref, lam_ref, goa_ref, gob_ref, w_out_ref,
                         x1_ref, xb_ref, h_ref, v_ref, *, d_a, d_b, seq):
    x = x_ref[...]
    u, v, xb, gate = _mixer_front(x, g_mix_ref[...], w_in_ref[...], g_v_ref[...], d_a, d_b)
    ya = _gmlp(u, v, wmix_ref, bias_ref[...], seq)

    cw = cw_ref[...]
    xc = cb_ref[...] + xb * cw[CONV_WIDTH - 1:CONV_WIDTH, :]
    pos = lax.broadcasted_iota(jnp.int32, xb.shape, 0) & (seq - 1)
    for k in range(1, CONV_WIDTH):
        shifted = jnp.where(pos < k, cs_ref[k - 1], pltpu.roll(xb, k, 0))
        xc = xc + shifted * cw[CONV_WIDTH - 1 - k:CONV_WIDTH - k, :]

    a, bx = _lru_inputs(xc, wa_ref[...], ba_ref[...], wi_ref[...], bi_ref[...], lam_ref[...])
    a_cum, b_cum = _seg_scan(a, bx, seq)
    h = a_cum * h0_ref[...] + b_cum

    x1_ref[...] = _mixer_back(x, ya, h, gate, goa_ref[...], gob_ref[...], w_out_ref, d_a)
    xb_ref[...] = xb
    h_ref[...] = h
    v_ref[...] = v


def _extract_top(s, vals_ref, head, lane0):
    keys = s.shape[0]
    key_idx = lax.broadcasted_iota(jnp.int32, s.shape, 0).astype(jnp.float32)

    def body(rank, s):
        m = jnp.max(s, axis=0, keepdims=True)
        first = jnp.min(jnp.where(s == m, key_idx, float(keys)), axis=0, keepdims=True)
        vals_ref[rank, pl.ds(head, 1), pl.ds(lane0, LANES)] = m
        code = -(RANK_BASE + rank.astype(jnp.float32) * RANK_STEP)
        return jnp.where(key_idx == first, code, s)

    return lax.fori_loop(0, PEER_TOPK, body, s)


def _decode_rank(s):
    return jnp.where(s < -0.5 * RANK_BASE, (-s - RANK_BASE) * (1.0 / RANK_STEP), float(PEER_TOPK))


def _staircase():
    return [(a, b) for a in range(PEER_TOPK) for b in range(PEER_TOPK) if (a + 1) * (b + 1) <= PEER_TOPK]


def _select_kernel(x_ref, g_ffn_ref, wq_ref, k1_ref, k2_ref,
                   rank2_ref, e2_ref, lim_ref, e1m_ref,
                   q_scr, rank1_scr, e1_scr, v1_scr, v2_scr, cand_scr, *, d_half):
    ts = x_ref.shape[0]
    h2 = _rms(x_ref[...], g_ffn_ref[...])
    q_scr[...] = _mm(h2, wq_ref[...]).astype(q_scr.dtype)

    for head in range(PEER_HEADS):
        for half, (k_ref, vals_scr) in enumerate(((k1_ref, v1_scr), (k2_ref, v2_scr))):
            col = (head * 2 + half) * d_half
            st = _mm_nt(k_ref[...], q_scr[:, col:col + d_half])
            e = jnp.exp(st - jnp.max(st, axis=0, keepdims=True))
            for slab in range(ts // LANES):
                lane0 = slab * LANES
                coded = _extract_top(st[:, lane0:lane0 + LANES], vals_scr, head, lane0)
                rank = _decode_rank(coded)
                if half == 0:
                    rank1_scr[head, :, lane0:lane0 + LANES] = rank
                else:
                    rank2_ref[head, :, lane0:lane0 + LANES] = rank.astype(rank2_ref.dtype)
            if half == 0:
                e1_scr[head] = e
            else:
                e2_ref[head] = e.astype(e2_ref.dtype)

    v1 = [v1_scr[a] for a in range(PEER_TOPK)]
    v2 = [v2_scr[b] for b in range(PEER_TOPK)]
    cands = _staircase()
    for c, (a, b) in enumerate(cands):
        cand_scr[c] = v1[a] + v2[b]
    cand_idx = lax.broadcasted_iota(jnp.int32, cand_scr.shape, 0).astype(jnp.float32)

    def pick_best(_, csum):
        m = jnp.max(csum, axis=0, keepdims=True)
        first = jnp.min(jnp.where(csum == m, cand_idx, float(len(cands))), axis=0, keepdims=True)
        return jnp.where(cand_idx == first, -jnp.inf, csum)

    cand_scr[...] = lax.fori_loop(0, PEER_TOPK, pick_best, cand_scr[...])

    ex1 = [jnp.exp(v1[a] - v1[0]) for a in range(PEER_TOPK)]
    ex2 = [jnp.exp(v2[b] - v2[0]) for b in range(PEER_TOPK)]
    z = jnp.zeros(v1[0].shape, jnp.float32)
    limit = [jnp.zeros(v1[0].shape, jnp.float32) for _ in range(PEER_TOPK)]
    for c, (a, b) in enumerate(cands):
        chosen = cand_scr[c] == -jnp.inf
        z = z + jnp.where(chosen, ex1[a] * ex2[b], 0.0)
        limit[a] = limit[a] + jnp.where(chosen, 1.0, 0.0)
    inv_z = 1.0 / z

    for head in range(PEER_HEADS):
        rank1 = rank1_scr[head]
        lim = jnp.zeros(rank1.shape, jnp.float32)
        for a in range(PEER_TOPK):
            lim = jnp.where(rank1 == float(a), limit[a][head:head + 1, :], lim)
        lim_ref[head] = lim
        e1m_ref[head] = e1_scr[head] * inv_z[head:head + 1, :]


def _expert_kernel(x_ref, g_ffn_ref, g_fin_ref, rank2_ref, e2_ref, lim_ref, e1m_ref, u_ref, vt_ref,
                   y_ref, h2_scr, act_scr, coef_scr, acc_scr):
    eb = pl.program_id(1)
    tokens = x_ref.shape[0]
    experts = u_ref.shape[0]
    keys = rank2_ref.shape[1]

    @pl.when(eb == 0)
    def _():
        h2_scr[...] = _rms(x_ref[...], g_ffn_ref[...]).astype(h2_scr.dtype)
        acc_scr[...] = jnp.zeros_like(acc_scr)

    act_scr[...] = jax.nn.gelu(_mm_nt(u_ref[...], h2_scr[...]))

    for ii in range(experts // keys):
        gates = jnp.zeros((keys, tokens), GATE_DTYPE)
        for head in range(PEER_HEADS):
            lim = jnp.broadcast_to(lim_ref[head, ii:ii + 1, :], (keys, tokens)).astype(GATE_DTYPE)
            e1 = jnp.broadcast_to(e1m_ref[head, ii:ii + 1, :], (keys, tokens)).astype(GATE_DTYPE)
            e2 = e2_ref[head]
            gates = gates + e1 * jnp.where(rank2_ref[head] < lim, e2, jnp.zeros_like(e2))
        rows = slice(ii * keys, (ii + 1) * keys)
        coef_scr[rows, :] = (act_scr[rows, :] * gates.astype(jnp.float32)).astype(coef_scr.dtype)

    acc_scr[...] += jnp.dot(vt_ref[...], coef_scr[...], preferred_element_type=jnp.float32)

    @pl.when(eb == pl.num_programs(1) - 1)
    def _():
        x2 = x_ref[...] + acc_scr[...].T
        y_ref[...] = _rms(x2, g_fin_ref[...])


def _pick_tile(n, candidates):
    for c in candidates:
        if n % c == 0:
            return c
    raise ValueError(f"no tile in {candidates} divides {n}")


def _full(shape):
    return pl.BlockSpec(shape, lambda *_: (0,) * len(shape))


def _params(semantics):
    return pltpu.CompilerParams(dimension_semantics=semantics, vmem_limit_bytes=VMEM_LIMIT_BYTES)


def _block_diag(w):
    heads, di, dj = w.shape
    eye = jnp.eye(heads, dtype=w.dtype)
    return jnp.einsum("hij,hk->hikj", w, eye).reshape(heads * di, heads * dj)


def _mixer_weights(g_mix, w_in, g_v, w_s, b_s, conv_w, conv_b, w_a, b_a, w_i, b_i, lam, g_out_a, g_out_b, w_out,
                   chunk_len):
    reps = GMLP_CHUNK // chunk_len
    wmix = jnp.tile(w_s[:, :chunk_len, :chunk_len], (1, reps, reps))
    gd = g_v.shape[-1] // A_GROUPS
    bias = jnp.tile(jnp.repeat(jnp.transpose(b_s[:, :chunk_len]), gd, axis=1), (reps, 1))
    row = lambda p: p.reshape(1, -1)
    return (row(g_mix), w_in.astype(MXU_DTYPE), row(g_v), wmix, bias, conv_w, row(conv_b),
            _block_diag(w_a).astype(MXU_DTYPE), row(b_a), _block_diag(w_i).astype(MXU_DTYPE), row(b_i), row(lam),
            row(g_out_a), row(g_out_b), w_out.astype(MXU_DTYPE))


def _prompt_mixer(x, weights, d_a, d_b):
    batch, seq, d_model = x.shape
    tile = _pick_tile(seq, (512, 256, 128))
    x2 = x.reshape(batch * seq, d_model)
    n_t = seq // tile
    w_specs = [_full(w.shape) for w in weights]
    x1, xb_tail, h_tail = pl.pallas_call(
        functools.partial(_prompt_mixer_kernel, d_a=d_a, d_b=d_b),
        grid=(batch, n_t),
        in_specs=[pl.BlockSpec((tile, d_model), lambda b, t: (b * n_t + t, 0))] + w_specs,
        out_specs=[pl.BlockSpec((tile, d_model), lambda b, t: (b * n_t + t, 0)),
                   pl.BlockSpec((1, SUBLANES, d_b), lambda b, t: (b, 0, 0)),
                   pl.BlockSpec((1, SUBLANES, d_b), lambda b, t: (b, 0, 0))],
        out_shape=[jax.ShapeDtypeStruct((batch * seq, d_model), jnp.float32),
                   jax.ShapeDtypeStruct((batch, SUBLANES, d_b), jnp.float32),
                   jax.ShapeDtypeStruct((batch, SUBLANES, d_b), jnp.float32)],
        scratch_shapes=[pltpu.VMEM((SUBLANES, d_b), jnp.float32), pltpu.VMEM((1, d_b), jnp.float32)],
        compiler_params=_params(("arbitrary", "arbitrary")),
        name="prompt_mixer",
    )(x2, *weights)
    new_conv = xb_tail[:, SUBLANES - (CONV_WIDTH - 1):, :]
    new_lru = h_tail[:, SUBLANES - 1, :]
    return x1, new_conv, new_lru


def _sample_mixer(x, state_conv, state_lru, weights, d_a, d_b):
    batch, seq, d_model = x.shape
    n = batch * seq
    x2 = x.reshape(n, d_model)
    cs = []
    for k in range(1, CONV_WIDTH):
        pad = jnp.zeros((batch, seq - k, d_b), state_conv.dtype)
        cs.append(jnp.concatenate([state_conv[:, CONV_WIDTH - 1 - k:, :], pad], axis=1).reshape(n, d_b))
    cs = jnp.stack(cs)
    h0 = jnp.repeat(state_lru, seq, axis=0)
    args = (x2, cs, h0) + tuple(weights)
    x1, xb, h, v = pl.pallas_call(
        functools.partial(_sample_mixer_kernel, d_a=d_a, d_b=d_b, seq=seq),
        grid=(1,),
        in_specs=[_full(a.shape) for a in args],
        out_specs=[_full((n, d_model)), _full((n, d_b)), _full((n, d_b)), _full((n, d_a))],
        out_shape=[jax.ShapeDtypeStruct((n, d_model), jnp.float32),
                   jax.ShapeDtypeStruct((n, d_b), jnp.float32),
                   jax.ShapeDtypeStruct((n, d_b), jnp.float32),
                   jax.ShapeDtypeStruct((n, d_a), jnp.float32)],
        compiler_params=_params(("arbitrary",)),
        name="sample_mixer",
    )(*args)
    new_conv = xb.reshape(batch, seq, d_b)[:, seq - (CONV_WIDTH - 1):, :]
    new_lru = h.reshape(batch, seq, d_b)[:, seq - 1, :]
    return x1, new_conv, new_lru, v.reshape(batch, seq, d_a)


def _peer(x1, g_ffn, g_fin, w_q, k1, k2, u_tab, v_tab):
    n, d_model = x1.shape
    n_keys, d_half = k1.shape
    n_experts = u_tab.shape[0]
    row = lambda p: p.reshape(1, -1)
    g_ffn, g_fin = row(g_ffn), row(g_fin)
    wq = w_q.astype(MXU_DTYPE)
    k1b, k2b = k1.astype(MXU_DTYPE), k2.astype(MXU_DTYPE)
    ub = u_tab.astype(MXU_DTYPE)
    vtb = jnp.transpose(v_tab).astype(MXU_DTYPE)

    ts = _pick_tile(n, (256, 128))
    table = lambda dt: jax.ShapeDtypeStruct((PEER_HEADS, n_keys, n), dt)
    tile_spec = pl.BlockSpec((PEER_HEADS, n_keys, ts), lambda t: (0, 0, t))
    rank2, e2, lim, e1m = pl.pallas_call(
        functools.partial(_select_kernel, d_half=d_half),
        grid=(n // ts,),
        in_specs=[pl.BlockSpec((ts, d_model), lambda t: (t, 0)), _full(g_ffn.shape), _full(wq.shape),
                  _full(k1b.shape), _full(k2b.shape)],
        out_specs=[tile_spec] * 4,
        out_shape=[table(GATE_DTYPE), table(GATE_DTYPE), table(jnp.float32), table(jnp.float32)],
        scratch_shapes=[pltpu.VMEM((ts, wq.shape[1]), MXU_DTYPE),
                        pltpu.VMEM((PEER_HEADS, n_keys, ts), jnp.float32),
                        pltpu.VMEM((PEER_HEADS, n_keys, ts), jnp.float32),
                        pltpu.VMEM((PEER_TOPK, PEER_HEADS, ts), jnp.float32),
                        pltpu.VMEM((PEER_TOPK, PEER_HEADS, ts), jnp.float32),
                        pltpu.VMEM((len(_staircase()), PEER_HEADS, ts), jnp.float32)],
        compiler_params=_params(("arbitrary",)),
        name="peer_select",
    )(x1, g_ffn, wq, k1b, k2b)

    tb = _pick_tile(n, (768, 512, 384, 256, 128))
    i_blk = SUBLANES
    e_blk = i_blk * n_keys
    y = pl.pallas_call(
        _expert_kernel,
        grid=(n // tb, n_experts // e_blk),
        in_specs=[pl.BlockSpec((tb, d_model), lambda t, e: (t, 0)), _full(g_ffn.shape), _full(g_fin.shape),
                  pl.BlockSpec((PEER_HEADS, n_keys, tb), lambda t, e: (0, 0, t)),
                  pl.BlockSpec((PEER_HEADS, n_keys, tb), lambda t, e: (0, 0, t)),
                  pl.BlockSpec((PEER_HEADS, i_blk, tb), lambda t, e: (0, e, t)),
                  pl.BlockSpec((PEER_HEADS, i_blk, tb), lambda t, e: (0, e, t)),
                  pl.BlockSpec((e_blk, d_model), lambda t, e: (e, 0)),
                  pl.BlockSpec((d_model, e_blk), lambda t, e: (0, e))],
        out_specs=pl.BlockSpec((tb, d_model), lambda t, e: (t, 0)),
        out_shape=jax.ShapeDtypeStruct((n, d_model), jnp.float32),
        scratch_shapes=[pltpu.VMEM((tb, d_model), MXU_DTYPE),
                        pltpu.VMEM((e_blk, tb), jnp.float32),
                        pltpu.VMEM((e_blk, tb), MXU_DTYPE),
                        pltpu.VMEM((d_model, tb), jnp.float32)],
        compiler_params=_params(("arbitrary", "arbitrary")),
        name="peer_experts",
    )(x1, g_ffn, g_fin, rank2, e2, lim, e1m, ub, vtb)
    return y


def kernel(x_prompt, x_sample, state_conv, state_lru, g_mix, w_in, g_v, w_s, b_s, conv_w, conv_b, w_a, b_a, w_i, b_i, lam, g_out_a, g_out_b, w_out, g_ffn, w_q, k_sub1, k_sub2, u_tab, v_tab, g_final):
    depth = w_in.shape[0]
    assert depth == 1, "the fused final norm assumes a single layer"
    d_a = g_v.shape[-1]
    d_b = conv_b.shape[-1]
    pb, ps, d_model = x_prompt.shape
    sb, ss, _ = x_sample.shape
    l = 0
    layer_w = (g_mix[l], w_in[l], g_v[l], w_s[l], b_s[l], conv_w[l], conv_b[l], w_a[l], b_a[l], w_i[l], b_i[l], lam[l],
               g_out_a[l], g_out_b[l], w_out[l])
    xp1, conv_p, lru_p = _prompt_mixer(x_prompt, _mixer_weights(*layer_w, chunk_len=min(ps, GMLP_CHUNK)), d_a, d_b)
    xs1, conv_s, lru_s, v_s = _sample_mixer(x_sample, state_conv[l], state_lru[l],
                                            _mixer_weights(*layer_w, chunk_len=min(ss, GMLP_CHUNK)), d_a, d_b)
    x1 = jnp.concatenate([xp1, xs1], axis=0)
    y = _peer(x1, g_ffn[l], g_final, w_q[l], k_sub1[l], k_sub2[l], u_tab[l], v_tab[l])
    n_p = pb * ps
    return (y[:n_p].reshape(pb, ps, d_model), y[n_p:].reshape(sb, ss, d_model),
            conv_p[None], lru_p[None], conv_s[None], lru_s[None], v_s[None])
```

```python
import functools

import jax
import jax.numpy as jnp
from jax import lax
from jax.experimental import pallas as pl
from jax.experimental.pallas import tpu as pltpu

EPS = 1e-6
LRU_C = 8.0
CONV_WIDTH = 4
A_GROUPS = 4
PEER_HEADS = 8
PEER_TOPK = 16
GMLP_CHUNK = 128

LANES = 128
SUBLANES = 8
PACKED_ROWS = 16
VMEM_LIMIT_BYTES = 56 * 1024 * 1024

MXU_DTYPE = jnp.bfloat16
GATE_DTYPE = jnp.bfloat16

RANK_BASE = 2.0 ** 100
RANK_STEP = 2.0 ** 80

_NT = (((1,), (1,)), ((), ()))


def _rms(x, g):
    return x * lax.rsqrt(jnp.mean(x * x, axis=-1, keepdims=True) + EPS) * g


def _gelu_tanh(x):
    c2 = 2.0 * 0.7978845608028654
    return x / (1.0 + jnp.exp(x * (-c2 - (c2 * 0.044715) * (x * x))))


def _mm(a, b):
    return jnp.dot(a.astype(MXU_DTYPE), b.astype(MXU_DTYPE), preferred_element_type=jnp.float32)


def _mm_nt(a, b):
    return lax.dot_general(a.astype(MXU_DTYPE), b.astype(MXU_DTYPE), _NT, preferred_element_type=jnp.float32)


def _seg_scan(a, b, seg):
    rows = lax.broadcasted_iota(jnp.int32, a.shape, 0)
    pos = rows & (seg - 1)
    s = 1
    while s < seg:
        a_sh = pltpu.roll(a, s, 0)
        b_sh = pltpu.roll(b, s, 0)
        valid = pos >= s
        b = jnp.where(valid, a * b_sh + b, b)
        a = jnp.where(valid, a * a_sh, a)
        s *= 2
    return a, b


def _mixer_front(x, g_mix, w_in, g_v, d_a, d_b):
    h = _rms(x, g_mix)
    z = _mm(h, w_in)
    za = jax.nn.gelu(z[:, :2 * d_a])
    u = za[:, :d_a]
    v = _rms(za[:, d_a:], g_v)
    xb = z[:, 2 * d_a:2 * d_a + d_b]
    gate = z[:, 2 * d_a + d_b:]
    return u, v, xb, gate


def _gmlp(u, v, wmix_ref, bias, chunk_len):
    rows, d_a = v.shape
    gd = d_a // A_GROUPS
    r = lax.broadcasted_iota(jnp.int32, (GMLP_CHUNK, GMLP_CHUNK), 0)
    c = lax.broadcasted_iota(jnp.int32, (GMLP_CHUNK, GMLP_CHUNK), 1)
    mask = (r >= c) & ((r // chunk_len) == (c // chunk_len))
    vb = v.astype(MXU_DTYPE)
    blocks = []
    for blk in range(rows // GMLP_CHUNK):
        cols = []
        for g in range(A_GROUPS):
            w = jnp.where(mask, wmix_ref[g], 0.0).astype(MXU_DTYPE)
            vg = vb[blk * GMLP_CHUNK:(blk + 1) * GMLP_CHUNK, g * gd:(g + 1) * gd]
            cols.append(jnp.dot(w, vg, preferred_element_type=jnp.float32))
        blocks.append(jnp.concatenate(cols, axis=1) + bias)
    s = blocks[0] if len(blocks) == 1 else jnp.concatenate(blocks, axis=0)
    return u * s


def _lru_inputs(xc, wa, ba, wi, bi, lam):
    r = jax.nn.sigmoid(_mm(xc, wa) + ba)
    i = jax.nn.sigmoid(_mm(xc, wi) + bi)
    log_sig = jnp.minimum(lam, 0.0) - jnp.log1p(jnp.exp(-jnp.abs(lam)))
    log_a = LRU_C * r * log_sig
    a = jnp.exp(log_a)
    t = jnp.tanh(log_a)
    bx = jnp.sqrt(-2.0 * t / (1.0 - t)) * (i * xc)
    return a, bx


def _mixer_back(x, ya, h, gate, g_out_a, g_out_b, w_out_ref, d_a):
    yb = h * jax.nn.gelu(gate)
    mix = _mm(_rms(ya, g_out_a), w_out_ref[:d_a, :]) + _mm(_rms(yb, g_out_b), w_out_ref[d_a:, :])
    return x + mix


def _prompt_mixer_kernel(x_ref, g_mix_ref, w_in_ref, g_v_ref, wmix_ref, bias_ref, cw_ref, cb_ref, wa_ref, ba_ref,
                         wi_ref, bi_ref, lam_ref, goa_ref, gob_ref, w_out_ref,
                         x1_ref, xb_tail_ref, h_tail_ref, conv_carry, h_carry, *, d_a, d_b):
    t = pl.program_id(1)

    @pl.when(t == 0)
    def _():
        conv_carry[...] = jnp.zeros_like(conv_carry)
        h_carry[...] = jnp.zeros_like(h_carry)

    x = x_ref[...]
    rows = x.shape[0]
    u, v, xb, gate = _mixer_front(x, g_mix_ref[...], w_in_ref[...], g_v_ref[...], d_a, d_b)
    ya = _gmlp(u, v, wmix_ref, bias_ref[...], GMLP_CHUNK)

    cw = cw_ref[...]
    xc = cb_ref[...] + xb * cw[CONV_WIDTH - 1:CONV_WIDTH, :]
    prev = conv_carry[...]
    row8 = lax.broadcasted_iota(jnp.int32, (SUBLANES, d_b), 0)
    for k in range(1, CONV_WIDTH):
        rolled = pltpu.roll(xb, k, 0)
        head = jnp.where(row8 < k, pltpu.roll(prev, k, 0), rolled[:SUBLANES, :])
        shifted = jnp.concatenate([head, rolled[SUBLANES:, :]], axis=0)
        xc = xc + shifted * cw[CONV_WIDTH - 1 - k:CONV_WIDTH - k, :]
    conv_carry[...] = xb[rows - SUBLANES:, :]

    a, bx = _lru_inputs(xc, wa_ref[...], ba_ref[...], wi_ref[...], bi_ref[...], lam_ref[...])
    a_cum, b_cum = _seg_scan(a, bx, rows)
    h = a_cum * h_carry[...] + b_cum
    h_carry[...] = h[rows - 1:rows, :]

    x1_ref[...] = _mixer_back(x, ya, h, gate, goa_ref[...], gob_ref[...], w_out_ref, d_a)
    xb_tail_ref[0] = xb[rows - SUBLANES:, :]
    h_tail_ref[0] = h[rows - SUBLANES:, :]


def _sample_mixer_kernel(x_ref, cs_ref, h0_ref, g_mix_ref, w_in_ref, g_v_ref, wmix_ref, bias_ref, cw_ref, cb_ref,
                         wa_ref, ba_ref, wi_ref, bi_ref, lam_ref, goa_ref, gob_ref, w_out_ref,
                         x1_ref, xb_ref, h_ref, v_ref, *, d_a, d_b, seq):
    x = x_ref[...]
    u, v, xb, gate = _mixer_front(x, g_mix_ref[...], w_in_ref[...], g_v_ref[...], d_a, d_b)
    ya = _gmlp(u, v, wmix_ref, bias_ref[...], seq)

    cw = cw_ref[...]
    xc = cb_ref[...] + xb * cw[CONV_WIDTH - 1:CONV_WIDTH, :]
    pos = lax.broadcasted_iota(jnp.int32, xb.shape, 0) & (seq - 1)
    for k in range(1, CONV_WIDTH):
        shifted = jnp.where(pos < k, cs_ref[k - 1], pltpu.roll(xb, k, 0))
        xc = xc + shifted * cw[CONV_WIDTH - 1 - k:CONV_WIDTH - k, :]

    a, bx = _lru_inputs(xc, wa_ref[...], ba_ref[...], wi_ref[...], bi_ref[...], lam_ref[...])
    a_cum, b_cum = _seg_scan(a, bx, seq)
    h = a_cum * h0_ref[...] + b_cum

    x1_ref[...] = _mixer_back(x, ya, h, gate, goa_ref[...], gob_ref[...], w_out_ref, d_a)
    xb_ref[...] = xb
    h_ref[...] = h
    v_ref[...] = v


def _tree(op, xs):
    while len(xs) > 1:
        xs = [op(xs[i], xs[i + 1]) if i + 1 < len(xs) else xs[i] for i in range(0, len(xs), 2)]
    return xs[0]


def _col_allreduce(x, op):
    r = _tree(op, [x[k:k + SUBLANES] for k in range(0, x.shape[0], SUBLANES)])
    for shift in (4, 2, 1):
        r = op(r, pltpu.roll(r, shift, 0))
    return r


def _tile_rows(r, rows):
    return jnp.concatenate([r] * (rows // r.shape[0]), axis=0)


def _grade_code(rank):
    return -(RANK_BASE + (float(PEER_TOPK) - rank) * RANK_STEP)


def _grades(coded):
    return jnp.maximum(coded * (-1.0 / RANK_STEP) - RANK_BASE / RANK_STEP, 0.0)


def _extract_top(chains, dests, stable):
    keys = chains[0].shape[0]
    key_idx = lax.broadcasted_iota(jnp.int32, chains[0].shape, 0).astype(jnp.float32)

    def body(rank, state):
        code = _grade_code(jnp.asarray(rank, jnp.float32))
        out = []
        for s, (vals_ref, head, lane0) in zip(state, dests):
            m = _col_allreduce(s, jnp.maximum)
            vals_ref[rank, pl.ds(head, 1), pl.ds(lane0, LANES)] = m[0:1]
            hit = s == _tile_rows(m, keys)
            if stable:
                first = _col_allreduce(jnp.where(hit, key_idx, float(keys)), jnp.minimum)
                hit = key_idx == _tile_rows(first, keys)
            out.append(jnp.where(hit, code, s))
        return tuple(out)

    if stable:
        return lax.fori_loop(0, PEER_TOPK, body, tuple(chains))
    state = tuple(chains)
    for rank in range(PEER_TOPK):
        state = body(rank, state)
    return state


def _staircase():
    return [(a, b) for a in range(PEER_TOPK) for b in range(PEER_TOPK) if (a + 1) * (b + 1) <= PEER_TOPK]


GRADE_SUM = float(PEER_TOPK * (PEER_TOPK + 1) // 2)


def _select_tables(q_scr, k1_ref, k2_ref, grade2_ref, e2_ref, cut_ref, e1m_ref,
                   grade1_scr, e1_scr, v1_scr, v2_scr, cand_scr, *, d_half, stable):
    ts = q_scr.shape[0]
    n_slab = ts // LANES
    suspect = jnp.zeros((SUBLANES, LANES), jnp.float32)

    for head in range(PEER_HEADS):
        for half, (k_ref, vals_scr) in enumerate(((k1_ref, v1_scr), (k2_ref, v2_scr))):
            col = (head * 2 + half) * d_half
            st = _mm_nt(k_ref[...], q_scr[:, col:col + d_half])
            if not stable:
                e = jnp.exp(st - jnp.max(st, axis=0, keepdims=True))
                if half == 0:
                    e1_scr[head] = e
                else:
                    e2_ref[head] = e.astype(e2_ref.dtype)
            chains = [st[:, slab * LANES:(slab + 1) * LANES] for slab in range(n_slab)]
            dests = [(vals_scr, head, slab * LANES) for slab in range(n_slab)]
            coded = _extract_top(chains, dests, stable)
            for slab, s in enumerate(coded):
                grade = _grades(s)
                if not stable:
                    total = _col_allreduce(grade, jnp.add)
                    suspect = jnp.maximum(suspect, jnp.where(total != GRADE_SUM, 1.0, 0.0))
                lanes = slice(slab * LANES, (slab + 1) * LANES)
                if half == 0:
                    grade1_scr[head, :, lanes] = grade
                else:
                    grade2_ref[head, :, lanes] = grade.astype(grade2_ref.dtype)

    v1 = [v1_scr[a] for a in range(PEER_TOPK)]
    v2 = [v2_scr[b] for b in range(PEER_TOPK)]
    cands = _staircase()
    for c, (a, b) in enumerate(cands):
        cand_scr[c] = v1[a] + v2[b]
    cand_idx = lax.broadcasted_iota(jnp.int32, cand_scr.shape, 0).astype(jnp.float32)

    def pick_best(_, csum):
        hit = csum == jnp.max(csum, axis=0, keepdims=True)
        if stable:
            first = jnp.min(jnp.where(hit, cand_idx, float(len(cands))), axis=0, keepdims=True)
            hit = cand_idx == first
        return jnp.where(hit, -jnp.inf, csum)

    cand_scr[...] = lax.fori_loop(0, PEER_TOPK, pick_best, cand_scr[...])

    ex1 = [jnp.exp(v1[a] - v1[0]) for a in range(PEER_TOPK)]
    ex2 = [jnp.exp(v2[b] - v2[0]) for b in range(PEER_TOPK)]
    z = jnp.zeros(v1[0].shape, jnp.float32)
    limit = [jnp.zeros(v1[0].shape, jnp.float32) for _ in range(PEER_TOPK)]
    for c, (a, b) in enumerate(cands):
        chosen = cand_scr[c] == -jnp.inf
        z = z + jnp.where(chosen, ex1[a] * ex2[b], 0.0)
        limit[a] = limit[a] + jnp.where(chosen, 1.0, 0.0)
    inv_z = 1.0 / z
    if not stable:
        n_chosen = _tree(jnp.add, list(limit))
        for slab in range(n_slab):
            lanes = slice(slab * LANES, (slab + 1) * LANES)
            suspect = jnp.maximum(suspect, jnp.where(n_chosen[:, lanes] != float(PEER_TOPK), 1.0, 0.0))

    for head in range(PEER_HEADS):
        grade1 = grade1_scr[head]
        cut = jnp.full(grade1.shape, float(PEER_TOPK), jnp.float32)
        for a in range(PEER_TOPK):
            cut = jnp.where(grade1 == float(PEER_TOPK - a), float(PEER_TOPK) - limit[a][head:head + 1, :], cut)
        cut_ref[head] = cut
        e1m_ref[head] = e1_scr[head] * inv_z[head:head + 1, :]
    return suspect


def _select_kernel(x_ref, g_ffn_ref, wq_ref, k1_ref, k2_ref,
                   grade2_ref, e2_ref, cut_ref, e1m_ref,
                   q_scr, grade1_scr, e1_scr, v1_scr, v2_scr, cand_scr, *, d_half):
    h2 = _rms(x_ref[...], g_ffn_ref[...])
    q_scr[...] = _mm(h2, wq_ref[...]).astype(q_scr.dtype)
    tables = functools.partial(_select_tables, q_scr, k1_ref, k2_ref, grade2_ref, e2_ref, cut_ref, e1m_ref,
                               grade1_scr, e1_scr, v1_scr, v2_scr, cand_scr, d_half=d_half)
    suspect = tables(stable=False)

    @pl.when(jnp.max(suspect) > 0.0)
    def _():
        tables(stable=True)


def _expert_kernel(x_ref, g_ffn_ref, g_fin_ref, grade2_ref, e2_ref, cut_ref, e1m_ref, u_ref, vt_ref,
                   y_ref, h2_scr, act_scr, coef_scr, acc_scr):
    eb = pl.program_id(1)
    tokens = x_ref.shape[0]
    experts = u_ref.shape[0]
    keys = grade2_ref.shape[1]

    @pl.when(eb == 0)
    def _():
        h2_scr[...] = _rms(x_ref[...], g_ffn_ref[...]).astype(h2_scr.dtype)
        acc_scr[...] = jnp.zeros_like(acc_scr)

    act_scr[...] = _gelu_tanh(_mm_nt(u_ref[...], h2_scr[...]))

    for ii in range(experts // keys):
        n_sub = keys // PACKED_ROWS
        gates = [jnp.zeros((PACKED_ROWS, tokens), GATE_DTYPE) for _ in range(n_sub)]
        for head in range(PEER_HEADS):
            cut = jnp.broadcast_to(cut_ref[head, ii:ii + 1, :], (PACKED_ROWS, tokens)).astype(GATE_DTYPE)
            e1 = jnp.broadcast_to(e1m_ref[head, ii:ii + 1, :], (PACKED_ROWS, tokens)).astype(GATE_DTYPE)
            for kk in range(n_sub):
                sub = slice(kk * PACKED_ROWS, (kk + 1) * PACKED_ROWS)
                picked = jnp.where(grade2_ref[head, sub, :] > cut, e2_ref[head, sub, :], jnp.zeros_like(cut))
                gates[kk] = gates[kk] + e1 * picked
        for kk in range(n_sub):
            rows = slice(ii * keys + kk * PACKED_ROWS, ii * keys + (kk + 1) * PACKED_ROWS)
            coef_scr[rows, :] = (act_scr[rows, :] * gates[kk].astype(jnp.float32)).astype(coef_scr.dtype)

    acc_scr[...] += jnp.dot(vt_ref[...], coef_scr[...], preferred_element_type=jnp.float32)

    @pl.when(eb == pl.num_programs(1) - 1)
    def _():
        x2 = x_ref[...] + acc_scr[...].T
        y_ref[...] = _rms(x2, g_fin_ref[...])


def _pick_tile(n, candidates):
    for c in candidates:
        if n % c == 0:
            return c
    raise ValueError(f"no tile in {candidates} divides {n}")


def _full(shape):
    return pl.BlockSpec(shape, lambda *_: (0,) * len(shape))


def _params(semantics):
    return pltpu.CompilerParams(dimension_semantics=semantics, vmem_limit_bytes=VMEM_LIMIT_BYTES)


def _block_diag(w):
    heads, di, dj = w.shape
    eye = jnp.eye(heads, dtype=w.dtype)
    return jnp.einsum("hij,hk->hikj", w, eye).reshape(heads * di, heads * dj)


def _mixer_weights(g_mix, w_in, g_v, w_s, b_s, conv_w, conv_b, w_a, b_a, w_i, b_i, lam, g_out_a, g_out_b, w_out,
                   chunk_len):
    reps = GMLP_CHUNK // chunk_len
    wmix = jnp.tile(w_s[:, :chunk_len, :chunk_len], (1, reps, reps))
    gd = g_v.shape[-1] // A_GROUPS
    bias = jnp.tile(jnp.repeat(jnp.transpose(b_s[:, :chunk_len]), gd, axis=1), (reps, 1))
    row = lambda p: p.reshape(1, -1)
    return (row(g_mix), w_in.astype(MXU_DTYPE), row(g_v), wmix, bias, conv_w, row(conv_b),
            _block_diag(w_a).astype(MXU_DTYPE), row(b_a), _block_diag(w_i).astype(MXU_DTYPE), row(b_i), row(lam),
            row(g_out_a), row(g_out_b), w_out.astype(MXU_DTYPE))


def _prompt_mixer(x, weights, d_a, d_b):
    batch, seq, d_model = x.shape
    tile = _pick_tile(seq, (512, 256, 128))
    x2 = x.reshape(batch * seq, d_model)
    n_t = seq // tile
    w_specs = [_full(w.shape) for w in weights]
    x1, xb_tail, h_tail = pl.pallas_call(
        functools.partial(_prompt_mixer_kernel, d_a=d_a, d_b=d_b),
        grid=(batch, n_t),
        in_specs=[pl.BlockSpec((tile, d_model), lambda b, t: (b * n_t + t, 0))] + w_specs,
        out_specs=[pl.BlockSpec((tile, d_model), lambda b, t: (b * n_t + t, 0)),
                   pl.BlockSpec((1, SUBLANES, d_b), lambda b, t: (b, 0, 0)),
                   pl.BlockSpec((1, SUBLANES, d_b), lambda b, t: (b, 0, 0))],
        out_shape=[jax.ShapeDtypeStruct((batch * seq, d_model), jnp.float32),
                   jax.ShapeDtypeStruct((batch, SUBLANES, d_b), jnp.float32),
                   jax.ShapeDtypeStruct((batch, SUBLANES, d_b), jnp.float32)],
        scratch_shapes=[pltpu.VMEM((SUBLANES, d_b), jnp.float32), pltpu.VMEM((1, d_b), jnp.float32)],
        compiler_params=_params(("arbitrary", "arbitrary")),
        name="prompt_mixer",
    )(x2, *weights)
    new_conv = xb_tail[:, SUBLANES - (CONV_WIDTH - 1):, :]
    new_lru = h_tail[:, SUBLANES - 1, :]
    return x1, new_conv, new_lru


def _sample_mixer(x, state_conv, state_lru, weights, d_a, d_b):
    batch, seq, d_model = x.shape
    n = batch * seq
    x2 = x.reshape(n, d_model)
    cs = []
    for k in range(1, CONV_WIDTH):
        pad = jnp.zeros((batch, seq - k, d_b), state_conv.dtype)
        cs.append(jnp.concatenate([state_conv[:, CONV_WIDTH - 1 - k:, :], pad], axis=1).reshape(n, d_b))
    cs = jnp.stack(cs)
    h0 = jnp.repeat(state_lru, seq, axis=0)
    args = (x2, cs, h0) + tuple(weights)
    x1, xb, h, v = pl.pallas_call(
        functools.partial(_sample_mixer_kernel, d_a=d_a, d_b=d_b, seq=seq),
        grid=(1,),
        in_specs=[_full(a.shape) for a in args],
        out_specs=[_full((n, d_model)), _full((n, d_b)), _full((n, d_b)), _full((n, d_a))],
        out_shape=[jax.ShapeDtypeStruct((n, d_model), jnp.float32),
                   jax.ShapeDtypeStruct((n, d_b), jnp.float32),
                   jax.ShapeDtypeStruct((n, d_b), jnp.float32),
                   jax.ShapeDtypeStruct((n, d_a), jnp.float32)],
        compiler_params=_params(("arbitrary",)),
        name="sample_mixer",
    )(*args)
    new_conv = xb.reshape(batch, seq, d_b)[:, seq - (CONV_WIDTH - 1):, :]
    new_lru = h.reshape(batch, seq, d_b)[:, seq - 1, :]
    return x1, new_conv, new_lru, v.reshape(batch, seq, d_a)


def _peer(x1, g_ffn, g_fin, w_q, k1, k2, u_tab, v_tab):
    n, d_model = x1.shape
    n_keys, d_half = k1.shape
    n_experts = u_tab.shape[0]
    row = lambda p: p.reshape(1, -1)
    g_ffn, g_fin = row(g_ffn), row(g_fin)
    wq = w_q.astype(MXU_DTYPE)
    k1b, k2b = k1.astype(MXU_DTYPE), k2.astype(MXU_DTYPE)
    ub = u_tab.astype(MXU_DTYPE)
    vtb = jnp.transpose(v_tab).astype(MXU_DTYPE)

    ts = _pick_tile(n, (256, 128))
    table = lambda dt: jax.ShapeDtypeStruct((PEER_HEADS, n_keys, n), dt)
    tile_spec = pl.BlockSpec((PEER_HEADS, n_keys, ts), lambda t: (0, 0, t))
    grade2, e2, cut, e1m = pl.pallas_call(
        functools.partial(_select_kernel, d_half=d_half),
        grid=(n // ts,),
        in_specs=[pl.BlockSpec((ts, d_model), lambda t: (t, 0)), _full(g_ffn.shape), _full(wq.shape),
                  _full(k1b.shape), _full(k2b.shape)],
        out_specs=[tile_spec] * 4,
        out_shape=[table(GATE_DTYPE), table(GATE_DTYPE), table(jnp.float32), table(jnp.float32)],
        scratch_shapes=[pltpu.VMEM((ts, wq.shape[1]), MXU_DTYPE),
                        pltpu.VMEM((PEER_HEADS, n_keys, ts), jnp.float32),
                        pltpu.VMEM((PEER_HEADS, n_keys, ts), jnp.float32),
                        pltpu.VMEM((PEER_TOPK, PEER_HEADS, ts), jnp.float32),
                        pltpu.VMEM((PEER_TOPK, PEER_HEADS, ts), jnp.float32),
                        pltpu.VMEM((len(_staircase()), PEER_HEADS, ts), jnp.float32)],
        compiler_params=_params(("arbitrary",)),
        name="peer_select",
    )(x1, g_ffn, wq, k1b, k2b)

    tb = _pick_tile(n, (768, 512, 384, 256, 128))
    i_blk = SUBLANES
    e_blk = i_blk * n_keys
    y = pl.pallas_call(
        _expert_kernel,
        grid=(n // tb, n_experts // e_blk),
        in_specs=[pl.BlockSpec((tb, d_model), lambda t, e: (t, 0)), _full(g_ffn.shape), _full(g_fin.shape),
                  pl.BlockSpec((PEER_HEADS, n_keys, tb), lambda t, e: (0, 0, t)),
                  pl.BlockSpec((PEER_HEADS, n_keys, tb), lambda t, e: (0, 0, t)),
                  pl.BlockSpec((PEER_HEADS, i_blk, tb), lambda t, e: (0, e, t)),
                  pl.BlockSpec((PEER_HEADS, i_blk, tb), lambda t, e: (0, e, t)),
                  pl.BlockSpec((e_blk, d_model), lambda t, e: (e, 0)),
                  pl.BlockSpec((d_model, e_blk), lambda t, e: (0, e))],
        out_specs=pl.BlockSpec((tb, d_model), lambda t, e: (t, 0)),
        out_shape=jax.ShapeDtypeStruct((n, d_model), jnp.float32),
        scratch_shapes=[pltpu.VMEM((tb, d_model), MXU_DTYPE),
                        pltpu.VMEM((e_blk, tb), jnp.float32),
                        pltpu.VMEM((e_blk, tb), MXU_DTYPE),
                        pltpu.VMEM((d_model, tb), jnp.float32)],
        compiler_params=_params(("arbitrary", "arbitrary")),
        name="peer_experts",
    )(x1, g_ffn, g_fin, grade2, e2, cut, e1m, ub, vtb)
    return y


def kernel(x_prompt, x_sample, state_conv, state_lru, g_mix, w_in, g_v, w_s, b_s, conv_w, conv_b, w_a, b_a, w_i, b_i, lam, g_out_a, g_out_b, w_out, g_ffn, w_q, k_sub1, k_sub2, u_tab, v_tab, g_final):
    depth = w_in.shape[0]
    assert depth == 1, "the fused final norm assumes a single layer"
    d_a = g_v.shape[-1]
    d_b = conv_b.shape[-1]
    pb, ps, d_model = x_prompt.shape
    sb, ss, _ = x_sample.shape
    l = 0
    layer_w = (g_mix[l], w_in[l], g_v[l], w_s[l], b_s[l], conv_w[l], conv_b[l], w_a[l], b_a[l], w_i[l], b_i[l], lam[l],
               g_out_a[l], g_out_b[l], w_out[l])
    xp1, conv_p, lru_p = _prompt_mixer(x_prompt, _mixer_weights(*layer_w, chunk_len=min(ps, GMLP_CHUNK)), d_a, d_b)
    xs1, conv_s, lru_s, v_s = _sample_mixer(x_sample, state_conv[l], state_lru[l],
                                            _mixer_weights(*layer_w, chunk_len=min(ss, GMLP_CHUNK)), d_a, d_b)
    x1 = jnp.concatenate([xp1, xs1], axis=0)
    y = _peer(x1, g_ffn[l], g_final, w_q[l], k_sub1[l], k_sub2[l], u_tab[l], v_tab[l])
    n_p = pb * ps
    return (y[:n_p].reshape(pb, ps, d_model), y[n_p:].reshape(sb, ss, d_model),
            conv_p[None], lru_p[None], conv_s[None], lru_s[None], v_s[None])
```

```python
import functools

import jax
import jax.numpy as jnp
from jax import lax
from jax.experimental import pallas as pl
from jax.experimental.pallas import tpu as pltpu

EPS = 1e-6
LRU_C = 8.0
CONV_WIDTH = 4
A_GROUPS = 4
PEER_HEADS = 8
PEER_TOPK = 16
GMLP_CHUNK = 128

LANES = 128
SUBLANES = 8
PACKED_ROWS = 16
VMEM_LIMIT_BYTES = 56 * 1024 * 1024

MXU_DTYPE = jnp.bfloat16
GATE_DTYPE = jnp.bfloat16

RANK_BASE = 2.0 ** 100
RANK_STEP = 2.0 ** 80

_NT = (((1,), (1,)), ((), ()))


def _rms(x, g):
    return x * lax.rsqrt(jnp.mean(x * x, axis=-1, keepdims=True) + EPS) * g


def _gelu_tanh(x):
    c2 = 2.0 * 0.7978845608028654
    return x / (1.0 + jnp.exp(x * (-c2 - (c2 * 0.044715) * (x * x))))


def _mm(a, b):
    return jnp.dot(a.astype(MXU_DTYPE), b.astype(MXU_DTYPE), preferred_element_type=jnp.float32)


def _mm_nt(a, b):
    return lax.dot_general(a.astype(MXU_DTYPE), b.astype(MXU_DTYPE), _NT, preferred_element_type=jnp.float32)


def _seg_scan(a, b, seg):
    rows = lax.broadcasted_iota(jnp.int32, a.shape, 0)
    pos = rows & (seg - 1)
    s = 1
    while s < seg:
        a_sh = pltpu.roll(a, s, 0)
        b_sh = pltpu.roll(b, s, 0)
        valid = pos >= s
        b = jnp.where(valid, a * b_sh + b, b)
        a = jnp.where(valid, a * a_sh, a)
        s *= 2
    return a, b


def _mixer_front(x, g_mix, w_in, g_v, d_a, d_b):
    h = _rms(x, g_mix)
    z = _mm(h, w_in)
    za = jax.nn.gelu(z[:, :2 * d_a])
    u = za[:, :d_a]
    v = _rms(za[:, d_a:], g_v)
    xb = z[:, 2 * d_a:2 * d_a + d_b]
    gate = z[:, 2 * d_a + d_b:]
    return u, v, xb, gate


def _gmlp(u, v, wmix_ref, bias, chunk_len):
    rows, d_a = v.shape
    gd = d_a // A_GROUPS
    r = lax.broadcasted_iota(jnp.int32, (GMLP_CHUNK, GMLP_CHUNK), 0)
    c = lax.broadcasted_iota(jnp.int32, (GMLP_CHUNK, GMLP_CHUNK), 1)
    mask = (r >= c) & ((r // chunk_len) == (c // chunk_len))
    vb = v.astype(MXU_DTYPE)
    blocks = []
    for blk in range(rows // GMLP_CHUNK):
        cols = []
        for g in range(A_GROUPS):
            w = jnp.where(mask, wmix_ref[g], 0.0).astype(MXU_DTYPE)
            vg = vb[blk * GMLP_CHUNK:(blk + 1) * GMLP_CHUNK, g * gd:(g + 1) * gd]
            cols.append(jnp.dot(w, vg, preferred_element_type=jnp.float32))
        blocks.append(jnp.concatenate(cols, axis=1) + bias)
    s = blocks[0] if len(blocks) == 1 else jnp.concatenate(blocks, axis=0)
    return u * s


def _lru_inputs(xc, wa, ba, wi, bi, lam):
    r = jax.nn.sigmoid(_mm(xc, wa) + ba)
    i = jax.nn.sigmoid(_mm(xc, wi) + bi)
    log_sig = jnp.minimum(lam, 0.0) - jnp.log1p(jnp.exp(-jnp.abs(lam)))
    log_a = LRU_C * r * log_sig
    a = jnp.exp(log_a)
    t = jnp.tanh(log_a)
    bx = jnp.sqrt(-2.0 * t / (1.0 - t)) * (i * xc)
    return a, bx


def _mixer_back(x, ya, h, gate, g_out_a, g_out_b, w_out_ref, d_a):
    yb = h * jax.nn.gelu(gate)
    mix = _mm(_rms(ya, g_out_a), w_out_ref[:d_a, :]) + _mm(_rms(yb, g_out_b), w_out_ref[d_a:, :])
    return x + mix


def _prompt_mixer_kernel(x_ref, g_mix_ref, w_in_ref, g_v_ref, wmix_ref, bias_ref, cw_ref, cb_ref, wa_ref, ba_ref,
                         wi_ref, bi_ref, lam_ref, goa_ref, gob_ref, w_out_ref,
                         x1_ref, xb_tail_ref, h_tail_ref, conv_carry, h_carry, *, d_a, d_b):
    t = pl.program_id(1)

    @pl.when(t == 0)
    def _():
        conv_carry[...] = jnp.zeros_like(conv_carry)
        h_carry[...] = jnp.zeros_like(h_carry)

    x = x_ref[...]
    rows = x.shape[0]
    u, v, xb, gate = _mixer_front(x, g_mix_ref[...], w_in_ref[...], g_v_ref[...], d_a, d_b)
    ya = _gmlp(u, v, wmix_ref, bias_ref[...], GMLP_CHUNK)

    cw = cw_ref[...]
    xc = cb_ref[...] + xb * cw[CONV_WIDTH - 1:CONV_WIDTH, :]
    prev = conv_carry[...]
    row8 = lax.broadcasted_iota(jnp.int32, (SUBLANES, d_b), 0)
    for k in range(1, CONV_WIDTH):
        rolled = pltpu.roll(xb, k, 0)
        head = jnp.where(row8 < k, pltpu.roll(prev, k, 0), rolled[:SUBLANES, :])
        shifted = jnp.concatenate([head, rolled[SUBLANES:, :]], axis=0)
        xc = xc + shifted * cw[CONV_WIDTH - 1 - k:CONV_WIDTH - k, :]
    conv_carry[...] = xb[rows - SUBLANES:, :]

    a, bx = _lru_inputs(xc, wa_ref[...], ba_ref[...], wi_ref[...], bi_ref[...], lam_ref[...])
    a_cum, b_cum = _seg_scan(a, bx, rows)
    h = a_cum * h_carry[...] + b_cum
    h_carry[...] = h[rows - 1:rows, :]

    x1_ref[...] = _mixer_back(x, ya, h, gate, goa_ref[...], gob_ref[...], w_out_ref, d_a)
    xb_tail_ref[0] = xb[rows - SUBLANES:, :]
    h_tail_ref[0] = h[rows - SUBLANES:, :]


def _sample_mixer_kernel(x_ref, cs_ref, h0_ref, g_mix_ref, w_in_ref, g_v_ref, wmix_ref, bias_ref, cw_ref, cb_ref,
                         wa_ref, ba_ref, wi_ref, bi_ref, lam_ref, goa_ref, gob_ref, w_out_ref,
                         x1_ref, xb_ref, h_ref, v_ref, *, d_a, d_b, seq):
    x = x_ref[...]
    u, v, xb, gate = _mixer_front(x, g_mix_ref[...], w_in_ref[...], g_v_ref[...], d_a, d_b)
    ya = _gmlp(u, v, wmix_ref, bias_ref[...], seq)

    cw = cw_ref[...]
    xc = cb_ref[...] + xb * cw[CONV_WIDTH - 1:CONV_WIDTH, :]
    pos = lax.broadcasted_iota(jnp.int32, xb.shape, 0) & (seq - 1)
    for k in range(1, CONV_WIDTH):
        shifted = jnp.where(pos < k, cs_ref[k - 1], pltpu.roll(xb, k, 0))
        xc = xc + shifted * cw[CONV_WIDTH - 1 - k:CONV_WIDTH - k, :]

    a, bx = _lru_inputs(xc, wa_ref[...], ba_ref[...], wi_ref[...], bi_ref[...], lam_ref[...])
    a_cum, b_cum = _seg_scan(a, bx, seq)
    h = a_cum * h0_ref[...] + b_cum

    x1_ref[...] = _mixer_back(x, ya, h, gate, goa_ref[...], gob_ref[...], w_out_ref, d_a)
    xb_ref[...] = xb
    h_ref[...] = h
    v_ref[...] = v


def _tree(op, xs):
    while len(xs) > 1:
        xs = [op(xs[i], xs[i + 1]) if i + 1 < len(xs) else xs[i] for i in range(0, len(xs), 2)]
    return xs[0]


def _col_allreduce(x, op):
    r = _tree(op, [x[k:k + SUBLANES] for k in range(0, x.shape[0], SUBLANES)])
    for shift in (4, 2, 1):
        r = op(r, pltpu.roll(r, shift, 0))
    return r


def _tile_rows(r, rows):
    return jnp.concatenate([r] * (rows // r.shape[0]), axis=0)


def _grade_code(rank):
    return -(RANK_BASE + (float(PEER_TOPK) - rank) * RANK_STEP)


def _grades(coded):
    return jnp.maximum(coded * (-1.0 / RANK_STEP) - RANK_BASE / RANK_STEP, 0.0)


def _extract_top(chains, dests, stable):
    keys = chains[0].shape[0]
    key_idx = lax.broadcasted_iota(jnp.int32, chains[0].shape, 0).astype(jnp.float32)

    def body(rank, state):
        code = _grade_code(jnp.asarray(rank, jnp.float32))
        out = []
        for s, (vals_ref, head, lane0) in zip(state, dests):
            m = _col_allreduce(s, jnp.maximum)
            vals_ref[rank, pl.ds(head, 1), pl.ds(lane0, LANES)] = m[0:1]
            hit = s == _tile_rows(m, keys)
            if stable:
                first = _col_allreduce(jnp.where(hit, key_idx, float(keys)), jnp.minimum)
                hit = key_idx == _tile_rows(first, keys)
            out.append(jnp.where(hit, code, s))
        return tuple(out)

    if stable:
        return lax.fori_loop(0, PEER_TOPK, body, tuple(chains))
    state = tuple(chains)
    for rank in range(PEER_TOPK):
        state = body(rank, state)
    return state


def _staircase():
    return [(a, b) for a in range(PEER_TOPK) for b in range(PEER_TOPK) if (a + 1) * (b + 1) <= PEER_TOPK]


GRADE_SUM = float(PEER_TOPK * (PEER_TOPK + 1) // 2)


def _head_top(head, q_scr, k_refs, vals_scrs, grade1_scr, grade2_ref, e1_scr, e2_ref, *, d_half, stable):
    ts = q_scr.shape[0]
    n_slab = ts // LANES
    suspect = jnp.zeros((SUBLANES, LANES), jnp.float32)
    for half, (k_ref, vals_scr) in enumerate(zip(k_refs, vals_scrs)):
        col = (head * 2 + half) * d_half
        st = _mm_nt(k_ref[...], q_scr[:, col:col + d_half])
        if not stable:
            e = jnp.exp(st - jnp.max(st, axis=0, keepdims=True))
            if half == 0:
                e1_scr[head] = e
            else:
                e2_ref[head] = e.astype(e2_ref.dtype)
        chains = [st[:, slab * LANES:(slab + 1) * LANES] for slab in range(n_slab)]
        dests = [(vals_scr, head, slab * LANES) for slab in range(n_slab)]
        coded = _extract_top(chains, dests, stable)
        for slab, s in enumerate(coded):
            grade = _grades(s)
            if not stable:
                total = _col_allreduce(grade, jnp.add)
                suspect = jnp.maximum(suspect, jnp.where(total != GRADE_SUM, 1.0, 0.0))
            lanes = slice(slab * LANES, (slab + 1) * LANES)
            if half == 0:
                grade1_scr[head, :, lanes] = grade
            else:
                grade2_ref[head, :, lanes] = grade.astype(grade2_ref.dtype)
    return suspect


def _chosen_candidates(v1, v2):
    cands = _staircase()
    csum = [v1[a] + v2[b] for a, b in cands]
    ahead = [[] for _ in cands]
    behind = [[] for _ in cands]
    for x, (a, b) in enumerate(cands):
        for y in range(x + 1, len(cands)):
            if a <= cands[y][0] and b <= cands[y][1]:
                continue
            first = jnp.where(csum[x] >= csum[y], 1.0, 0.0)
            ahead[y].append(first)
            behind[x].append(first)
    chosen = []
    for y, (a, b) in enumerate(cands):
        dominators = (a + 1) * (b + 1) - 1
        rank = float(dominators + len(behind[y]))
        if ahead[y]:
            rank = rank + _tree(jnp.add, ahead[y])
        if behind[y]:
            rank = rank - _tree(jnp.add, behind[y])
        chosen.append(rank < float(PEER_TOPK))
    return cands, chosen


def _select_kernel(x_ref, g_ffn_ref, wq_ref, k1_ref, k2_ref,
                   grade2_ref, e2_ref, cut_ref, e1m_ref,
                   q_scr, grade1_scr, e1_scr, v1_scr, v2_scr, *, d_half):
    h2 = _rms(x_ref[...], g_ffn_ref[...])
    q_scr[...] = _mm(h2, wq_ref[...]).astype(q_scr.dtype)
    head_top = functools.partial(_head_top, q_scr=q_scr, k_refs=(k1_ref, k2_ref), vals_scrs=(v1_scr, v2_scr),
                                 grade1_scr=grade1_scr, grade2_ref=grade2_ref, e1_scr=e1_scr, e2_ref=e2_ref,
                                 d_half=d_half)
    redo = [jnp.max(head_top(head, stable=False)) > 0.0 for head in range(PEER_HEADS)]
    for head in range(PEER_HEADS):
        @pl.when(redo[head])
        def _(head=head):
            head_top(head, stable=True)

    v1 = [v1_scr[a] for a in range(PEER_TOPK)]
    v2 = [v2_scr[b] for b in range(PEER_TOPK)]
    cands, chosen = _chosen_candidates(v1, v2)
    ex1 = [jnp.exp(v1[a] - v1[0]) for a in range(PEER_TOPK)]
    ex2 = [jnp.exp(v2[b] - v2[0]) for b in range(PEER_TOPK)]
    z = jnp.zeros(v1[0].shape, jnp.float32)
    limit = [jnp.zeros(v1[0].shape, jnp.float32) for _ in range(PEER_TOPK)]
    for c, (a, b) in enumerate(cands):
        z = z + jnp.where(chosen[c], ex1[a] * ex2[b], 0.0)
        limit[a] = limit[a] + jnp.where(chosen[c], 1.0, 0.0)
    inv_z = 1.0 / z

    for head in range(PEER_HEADS):
        grade1 = grade1_scr[head]
        cut = jnp.full(grade1.shape, float(PEER_TOPK), jnp.float32)
        for a in range(PEER_TOPK):
            cut = jnp.where(grade1 == float(PEER_TOPK - a), float(PEER_TOPK) - limit[a][head:head + 1, :], cut)
        cut_ref[head] = cut
        e1m_ref[head] = e1_scr[head] * inv_z[head:head + 1, :]


def _expert_kernel(x_ref, g_ffn_ref, g_fin_ref, grade2_ref, e2_ref, cut_ref, e1m_ref, u_ref, vt_ref,
                   y_ref, h2_scr, act_scr, coef_scr, acc_scr):
    eb = pl.program_id(1)
    tokens = x_ref.shape[0]
    experts = u_ref.shape[0]
    keys = grade2_ref.shape[1]

    @pl.when(eb == 0)
    def _():
        h2_scr[...] = _rms(x_ref[...], g_ffn_ref[...]).astype(h2_scr.dtype)
        acc_scr[...] = jnp.zeros_like(acc_scr)

    act_scr[...] = _gelu_tanh(_mm_nt(u_ref[...], h2_scr[...]))

    n_sub = keys // PACKED_ROWS
    for ii in range(experts // keys):
        gates = [jnp.zeros((PACKED_ROWS, tokens), GATE_DTYPE) for _ in range(n_sub)]
        for head in range(PEER_HEADS):
            cut = jnp.broadcast_to(cut_ref[head, ii:ii + 1, :], (PACKED_ROWS, tokens)).astype(GATE_DTYPE)
            e1 = jnp.broadcast_to(e1m_ref[head, ii:ii + 1, :], (PACKED_ROWS, tokens)).astype(GATE_DTYPE)
            for kk in range(n_sub):
                sub = slice(kk * PACKED_ROWS, (kk + 1) * PACKED_ROWS)
                picked = jnp.where(grade2_ref[head, sub, :] > cut, e2_ref[head, sub, :], jnp.zeros_like(cut))
                gates[kk] = gates[kk] + e1 * picked
        for kk in range(n_sub):
            rows = slice(ii * keys + kk * PACKED_ROWS, ii * keys + (kk + 1) * PACKED_ROWS)
            coef_scr[rows, :] = (act_scr[rows, :] * gates[kk].astype(jnp.float32)).astype(coef_scr.dtype)

    acc_scr[...] += jnp.dot(vt_ref[...], coef_scr[...], preferred_element_type=jnp.float32)

    @pl.when(eb == pl.num_programs(1) - 1)
    def _():
        x2 = x_ref[...] + acc_scr[...].T
        y_ref[...] = _rms(x2, g_fin_ref[...])


def _pick_tile(n, candidates):
    for c in candidates:
        if n % c == 0:
            return c
    raise ValueError(f"no tile in {candidates} divides {n}")


def _full(shape):
    return pl.BlockSpec(shape, lambda *_: (0,) * len(shape))


def _params(semantics):
    return pltpu.CompilerParams(dimension_semantics=semantics, vmem_limit_bytes=VMEM_LIMIT_BYTES)


def _block_diag(w):
    heads, di, dj = w.shape
    eye = jnp.eye(heads, dtype=w.dtype)
    return jnp.einsum("hij,hk->hikj", w, eye).reshape(heads * di, heads * dj)


def _mixer_weights(g_mix, w_in, g_v, w_s, b_s, conv_w, conv_b, w_a, b_a, w_i, b_i, lam, g_out_a, g_out_b, w_out,
                   chunk_len):
    reps = GMLP_CHUNK // chunk_len
    wmix = jnp.tile(w_s[:, :chunk_len, :chunk_len], (1, reps, reps))
    gd = g_v.shape[-1] // A_GROUPS
    bias = jnp.tile(jnp.repeat(jnp.transpose(b_s[:, :chunk_len]), gd, axis=1), (reps, 1))
    row = lambda p: p.reshape(1, -1)
    return (row(g_mix), w_in.astype(MXU_DTYPE), row(g_v), wmix, bias, conv_w, row(conv_b),
            _block_diag(w_a).astype(MXU_DTYPE), row(b_a), _block_diag(w_i).astype(MXU_DTYPE), row(b_i), row(lam),
            row(g_out_a), row(g_out_b), w_out.astype(MXU_DTYPE))


def _prompt_mixer(x, weights, d_a, d_b):
    batch, seq, d_model = x.shape
    tile = _pick_tile(seq, (512, 256, 128))
    x2 = x.reshape(batch * seq, d_model)
    n_t = seq // tile
    w_specs = [_full(w.shape) for w in weights]
    x1, xb_tail, h_tail = pl.pallas_call(
        functools.partial(_prompt_mixer_kernel, d_a=d_a, d_b=d_b),
        grid=(batch, n_t),
        in_specs=[pl.BlockSpec((tile, d_model), lambda b, t: (b * n_t + t, 0))] + w_specs,
        out_specs=[pl.BlockSpec((tile, d_model), lambda b, t: (b * n_t + t, 0)),
                   pl.BlockSpec((1, SUBLANES, d_b), lambda b, t: (b, 0, 0)),
                   pl.BlockSpec((1, SUBLANES, d_b), lambda b, t: (b, 0, 0))],
        out_shape=[jax.ShapeDtypeStruct((batch * seq, d_model), jnp.float32),
                   jax.ShapeDtypeStruct((batch, SUBLANES, d_b), jnp.float32),
                   jax.ShapeDtypeStruct((batch, SUBLANES, d_b), jnp.float32)],
        scratch_shapes=[pltpu.VMEM((SUBLANES, d_b), jnp.float32), pltpu.VMEM((1, d_b), jnp.float32)],
        compiler_params=_params(("arbitrary", "arbitrary")),
        name="prompt_mixer",
    )(x2, *weights)
    new_conv = xb_tail[:, SUBLANES - (CONV_WIDTH - 1):, :]
    new_lru = h_tail[:, SUBLANES - 1, :]
    return x1, new_conv, new_lru


def _sample_mixer(x, state_conv, state_lru, weights, d_a, d_b):
    batch, seq, d_model = x.shape
    n = batch * seq
    x2 = x.reshape(n, d_model)
    cs = []
    for k in range(1, CONV_WIDTH):
        pad = jnp.zeros((batch, seq - k, d_b), state_conv.dtype)
        cs.append(jnp.concatenate([state_conv[:, CONV_WIDTH - 1 - k:, :], pad], axis=1).reshape(n, d_b))
    cs = jnp.stack(cs)
    h0 = jnp.repeat(state_lru, seq, axis=0)
    args = (x2, cs, h0) + tuple(weights)
    x1, xb, h, v = pl.pallas_call(
        functools.partial(_sample_mixer_kernel, d_a=d_a, d_b=d_b, seq=seq),
        grid=(1,),
        in_specs=[_full(a.shape) for a in args],
        out_specs=[_full((n, d_model)), _full((n, d_b)), _full((n, d_b)), _full((n, d_a))],
        out_shape=[jax.ShapeDtypeStruct((n, d_model), jnp.float32),
                   jax.ShapeDtypeStruct((n, d_b), jnp.float32),
                   jax.ShapeDtypeStruct((n, d_b), jnp.float32),
                   jax.ShapeDtypeStruct((n, d_a), jnp.float32)],
        compiler_params=_params(("arbitrary",)),
        name="sample_mixer",
    )(*args)
    new_conv = xb.reshape(batch, seq, d_b)[:, seq - (CONV_WIDTH - 1):, :]
    new_lru = h.reshape(batch, seq, d_b)[:, seq - 1, :]
    return x1, new_conv, new_lru, v.reshape(batch, seq, d_a)


def _peer_weights(g_ffn, g_fin, w_q, k1, k2, u_tab, v_tab):
    row = lambda p: p.reshape(1, -1)
    vtb = jnp.transpose(v_tab).astype(MXU_DTYPE)
    return (row(g_ffn), row(g_fin), w_q.astype(MXU_DTYPE), k1.astype(MXU_DTYPE), k2.astype(MXU_DTYPE),
            u_tab.astype(MXU_DTYPE), vtb)


def _peer(x1, weights):
    g_ffn, g_fin, wq, k1b, k2b, ub, vtb = weights
    n, d_model = x1.shape
    n_keys, d_half = k1b.shape
    n_experts = ub.shape[0]

    ts = _pick_tile(n, (256, 128))
    table = lambda dt: jax.ShapeDtypeStruct((PEER_HEADS, n_keys, n), dt)
    tile_spec = pl.BlockSpec((PEER_HEADS, n_keys, ts), lambda t: (0, 0, t))
    grade2, e2, cut, e1m = pl.pallas_call(
        functools.partial(_select_kernel, d_half=d_half),
        grid=(n // ts,),
        in_specs=[pl.BlockSpec((ts, d_model), lambda t: (t, 0)), _full(g_ffn.shape), _full(wq.shape),
                  _full(k1b.shape), _full(k2b.shape)],
        out_specs=[tile_spec] * 4,
        out_shape=[table(GATE_DTYPE), table(GATE_DTYPE), table(jnp.float32), table(jnp.float32)],
        scratch_shapes=[pltpu.VMEM((ts, wq.shape[1]), MXU_DTYPE),
                        pltpu.VMEM((PEER_HEADS, n_keys, ts), jnp.float32),
                        pltpu.VMEM((PEER_HEADS, n_keys, ts), jnp.float32),
                        pltpu.VMEM((PEER_TOPK, PEER_HEADS, ts), jnp.float32),
                        pltpu.VMEM((PEER_TOPK, PEER_HEADS, ts), jnp.float32)],
        compiler_params=_params(("arbitrary",)),
        name="peer_select",
    )(x1, g_ffn, wq, k1b, k2b)

    tb = _pick_tile(n, (768, 512, 384, 256, 128))
    i_blk = SUBLANES
    e_blk = i_blk * n_keys
    y = pl.pallas_call(
        _expert_kernel,
        grid=(n // tb, n_experts // e_blk),
        in_specs=[pl.BlockSpec((tb, d_model), lambda t, e: (t, 0)), _full(g_ffn.shape), _full(g_fin.shape),
                  pl.BlockSpec((PEER_HEADS, n_keys, tb), lambda t, e: (0, 0, t)),
                  pl.BlockSpec((PEER_HEADS, n_keys, tb), lambda t, e: (0, 0, t)),
                  pl.BlockSpec((PEER_HEADS, i_blk, tb), lambda t, e: (0, e, t)),
                  pl.BlockSpec((PEER_HEADS, i_blk, tb), lambda t, e: (0, e, t)),
                  pl.BlockSpec((e_blk, d_model), lambda t, e: (e, 0)),
                  pl.BlockSpec((d_model, e_blk), lambda t, e: (0, e))],
        out_specs=pl.BlockSpec((tb, d_model), lambda t, e: (t, 0)),
        out_shape=jax.ShapeDtypeStruct((n, d_model), jnp.float32),
        scratch_shapes=[pltpu.VMEM((tb, d_model), MXU_DTYPE),
                        pltpu.VMEM((e_blk, tb), jnp.float32),
                        pltpu.VMEM((e_blk, tb), MXU_DTYPE),
                        pltpu.VMEM((d_model, tb), jnp.float32)],
        compiler_params=_params(("arbitrary", "arbitrary")),
        name="peer_experts",
    )(x1, g_ffn, g_fin, grade2, e2, cut, e1m, ub, vtb)
    return y


def kernel(x_prompt, x_sample, state_conv, state_lru, g_mix, w_in, g_v, w_s, b_s, conv_w, conv_b, w_a, b_a, w_i, b_i, lam, g_out_a, g_out_b, w_out, g_ffn, w_q, k_sub1, k_sub2, u_tab, v_tab, g_final):
    depth = w_in.shape[0]
    assert depth == 1, "the fused final norm assumes a single layer"
    d_a = g_v.shape[-1]
    d_b = conv_b.shape[-1]
    pb, ps, d_model = x_prompt.shape
    sb, ss, _ = x_sample.shape
    l = 0
    layer_w = (g_mix[l], w_in[l], g_v[l], w_s[l], b_s[l], conv_w[l], conv_b[l], w_a[l], b_a[l], w_i[l], b_i[l], lam[l],
               g_out_a[l], g_out_b[l], w_out[l])
    xp1, conv_p, lru_p = _prompt_mixer(x_prompt, _mixer_weights(*layer_w, chunk_len=min(ps, GMLP_CHUNK)), d_a, d_b)
    xs1, conv_s, lru_s, v_s = _sample_mixer(x_sample, state_conv[l], state_lru[l],
                                            _mixer_weights(*layer_w, chunk_len=min(ss, GMLP_CHUNK)), d_a, d_b)
    peer_w = _peer_weights(g_ffn[l], g_final, w_q[l], k_sub1[l], k_sub2[l], u_tab[l], v_tab[l])
    y = _peer(jnp.concatenate([xp1, xs1], axis=0), peer_w)
    n_p = pb * ps
    return (y[:n_p].reshape(pb, ps, d_model), y[n_p:].reshape(sb, ss, d_model),
            conv_p[None], lru_p[None], conv_s[None], lru_s[None], v_s[None])
```

```python
import functools

import jax
import jax.numpy as jnp
from jax import lax
from jax.experimental import pallas as pl
from jax.experimental.pallas import tpu as pltpu

EPS = 1e-6
LRU_C = 8.0
CONV_WIDTH = 4
A_GROUPS = 4
PEER_HEADS = 8
PEER_TOPK = 16
GMLP_CHUNK = 128

LANES = 128
SUBLANES = 8
PACKED_ROWS = 16
VMEM_LIMIT_BYTES = 56 * 1024 * 1024

MXU_DTYPE = jnp.bfloat16
GATE_DTYPE = jnp.bfloat16

RANK_BASE = 2.0 ** 100
RANK_STEP = 2.0 ** 80

_NT = (((1,), (1,)), ((), ()))


def _rms(x, g):
    return x * lax.rsqrt(jnp.mean(x * x, axis=-1, keepdims=True) + EPS) * g


def _gelu_tanh(x, dtype):
    c2 = 2.0 * 0.7978845608028654 * 1.4426950408889634
    z = x * (-c2 - (c2 * 0.044715) * (x * x))
    return x.astype(dtype) / (1.0 + jnp.exp2(z.astype(dtype)))


def _mm(a, b):
    return jnp.dot(a.astype(MXU_DTYPE), b.astype(MXU_DTYPE), preferred_element_type=jnp.float32)


def _mm_nt(a, b):
    return lax.dot_general(a.astype(MXU_DTYPE), b.astype(MXU_DTYPE), _NT, preferred_element_type=jnp.float32)


def _seg_scan(a, b, seg):
    rows = lax.broadcasted_iota(jnp.int32, a.shape, 0)
    pos = rows & (seg - 1)
    s = 1
    while s < seg:
        a_sh = pltpu.roll(a, s, 0)
        b_sh = pltpu.roll(b, s, 0)
        valid = pos >= s
        b = jnp.where(valid, a * b_sh + b, b)
        a = jnp.where(valid, a * a_sh, a)
        s *= 2
    return a, b


def _mixer_front(x, g_mix, w_in, g_v, d_a, d_b):
    h = _rms(x, g_mix)
    z = _mm(h, w_in)
    za = jax.nn.gelu(z[:, :2 * d_a])
    u = za[:, :d_a]
    v = _rms(za[:, d_a:], g_v)
    xb = z[:, 2 * d_a:2 * d_a + d_b]
    gate = z[:, 2 * d_a + d_b:]
    return u, v, xb, gate


def _gmlp(u, v, wmix_ref, bias, chunk_len):
    rows, d_a = v.shape
    gd = d_a // A_GROUPS
    r = lax.broadcasted_iota(jnp.int32, (GMLP_CHUNK, GMLP_CHUNK), 0)
    c = lax.broadcasted_iota(jnp.int32, (GMLP_CHUNK, GMLP_CHUNK), 1)
    mask = (r >= c) & ((r // chunk_len) == (c // chunk_len))
    vb = v.astype(MXU_DTYPE)
    blocks = []
    for blk in range(rows // GMLP_CHUNK):
        cols = []
        for g in range(A_GROUPS):
            w = jnp.where(mask, wmix_ref[g], 0.0).astype(MXU_DTYPE)
            vg = vb[blk * GMLP_CHUNK:(blk + 1) * GMLP_CHUNK, g * gd:(g + 1) * gd]
            cols.append(jnp.dot(w, vg, preferred_element_type=jnp.float32))
        blocks.append(jnp.concatenate(cols, axis=1) + bias)
    s = blocks[0] if len(blocks) == 1 else jnp.concatenate(blocks, axis=0)
    return u * s


def _lru_inputs(xc, wa, ba, wi, bi, lam):
    r = jax.nn.sigmoid(_mm(xc, wa) + ba)
    i = jax.nn.sigmoid(_mm(xc, wi) + bi)
    log_sig = jnp.minimum(lam, 0.0) - jnp.log1p(jnp.exp(-jnp.abs(lam)))
    log_a = LRU_C * r * log_sig
    a = jnp.exp(log_a)
    t = jnp.tanh(log_a)
    bx = jnp.sqrt(-2.0 * t / (1.0 - t)) * (i * xc)
    return a, bx


def _mixer_back(x, ya, h, gate, g_out_a, g_out_b, w_out_ref, d_a):
    yb = h * jax.nn.gelu(gate)
    mix = _mm(_rms(ya, g_out_a), w_out_ref[:d_a, :]) + _mm(_rms(yb, g_out_b), w_out_ref[d_a:, :])
    return x + mix


def _prompt_mixer_kernel(x_ref, g_mix_ref, w_in_ref, g_v_ref, wmix_ref, bias_ref, cw_ref, cb_ref, wa_ref, ba_ref,
                         wi_ref, bi_ref, lam_ref, goa_ref, gob_ref, w_out_ref,
                         x1_ref, xb_tail_ref, h_tail_ref, conv_carry, h_carry, *, d_a, d_b):
    t = pl.program_id(1)

    @pl.when(t == 0)
    def _():
        conv_carry[...] = jnp.zeros_like(conv_carry)
        h_carry[...] = jnp.zeros_like(h_carry)

    x = x_ref[...]
    rows = x.shape[0]
    u, v, xb, gate = _mixer_front(x, g_mix_ref[...], w_in_ref[...], g_v_ref[...], d_a, d_b)
    ya = _gmlp(u, v, wmix_ref, bias_ref[...], GMLP_CHUNK)

    cw = cw_ref[...]
    xc = cb_ref[...] + xb * cw[CONV_WIDTH - 1:CONV_WIDTH, :]
    prev = conv_carry[...]
    row8 = lax.broadcasted_iota(jnp.int32, (SUBLANES, d_b), 0)
    for k in range(1, CONV_WIDTH):
        rolled = pltpu.roll(xb, k, 0)
        head = jnp.where(row8 < k, pltpu.roll(prev, k, 0), rolled[:SUBLANES, :])
        shifted = jnp.concatenate([head, rolled[SUBLANES:, :]], axis=0)
        xc = xc + shifted * cw[CONV_WIDTH - 1 - k:CONV_WIDTH - k, :]
    conv_carry[...] = xb[rows - SUBLANES:, :]

    a, bx = _lru_inputs(xc, wa_ref[...], ba_ref[...], wi_ref[...], bi_ref[...], lam_ref[...])
    a_cum, b_cum = _seg_scan(a, bx, rows)
    h = a_cum * h_carry[...] + b_cum
    h_carry[...] = h[rows - 1:rows, :]

    x1_ref[...] = _mixer_back(x, ya, h, gate, goa_ref[...], gob_ref[...], w_out_ref, d_a)
    xb_tail_ref[0] = xb[rows - SUBLANES:, :]
    h_tail_ref[0] = h[rows - SUBLANES:, :]


def _sample_mixer_kernel(x_ref, cs_ref, h0_ref, g_mix_ref, w_in_ref, g_v_ref, wmix_ref, bias_ref, cw_ref, cb_ref,
                         wa_ref, ba_ref, wi_ref, bi_ref, lam_ref, goa_ref, gob_ref, w_out_ref,
                         x1_ref, xb_ref, h_ref, v_ref, *, d_a, d_b, seq):
    x = x_ref[...]
    u, v, xb, gate = _mixer_front(x, g_mix_ref[...], w_in_ref[...], g_v_ref[...], d_a, d_b)
    ya = _gmlp(u, v, wmix_ref, bias_ref[...], seq)

    cw = cw_ref[...]
    xc = cb_ref[...] + xb * cw[CONV_WIDTH - 1:CONV_WIDTH, :]
    pos = lax.broadcasted_iota(jnp.int32, xb.shape, 0) & (seq - 1)
    for k in range(1, CONV_WIDTH):
        shifted = jnp.where(pos < k, cs_ref[k - 1], pltpu.roll(xb, k, 0))
        xc = xc + shifted * cw[CONV_WIDTH - 1 - k:CONV_WIDTH - k, :]

    a, bx = _lru_inputs(xc, wa_ref[...], ba_ref[...], wi_ref[...], bi_ref[...], lam_ref[...])
    a_cum, b_cum = _seg_scan(a, bx, seq)
    h = a_cum * h0_ref[...] + b_cum

    x1_ref[...] = _mixer_back(x, ya, h, gate, goa_ref[...], gob_ref[...], w_out_ref, d_a)
    xb_ref[...] = xb
    h_ref[...] = h
    v_ref[...] = v


def _tree(op, xs):
    while len(xs) > 1:
        xs = [op(xs[i], xs[i + 1]) if i + 1 < len(xs) else xs[i] for i in range(0, len(xs), 2)]
    return xs[0]


def _col_allreduce(x, op):
    r = _tree(op, [x[k:k + SUBLANES] for k in range(0, x.shape[0], SUBLANES)])
    for shift in (4, 2, 1):
        r = op(r, pltpu.roll(r, shift, 0))
    return r


def _tile_rows(r, rows):
    return jnp.concatenate([r] * (rows // r.shape[0]), axis=0)


def _grade_code(rank):
    return -(RANK_BASE + (float(PEER_TOPK) - rank) * RANK_STEP)


def _grades(coded):
    return jnp.maximum(coded * (-1.0 / RANK_STEP) - RANK_BASE / RANK_STEP, 0.0)


def _extract_top(chains, dests, stable):
    keys = chains[0].shape[0]
    key_idx = lax.broadcasted_iota(jnp.int32, chains[0].shape, 0).astype(jnp.float32)

    def body(rank, state):
        code = _grade_code(jnp.asarray(rank, jnp.float32))
        out = []
        for s, (vals_ref, head, lane0) in zip(state, dests):
            m = _col_allreduce(s, jnp.maximum)
            vals_ref[rank, pl.ds(head, 1), pl.ds(lane0, LANES)] = m[0:1]
            hit = s == _tile_rows(m, keys)
            if stable:
                first = _col_allreduce(jnp.where(hit, key_idx, float(keys)), jnp.minimum)
                hit = key_idx == _tile_rows(first, keys)
            out.append(jnp.where(hit, code, s))
        return tuple(out)

    if stable:
        return lax.fori_loop(0, PEER_TOPK, body, tuple(chains))
    state = tuple(chains)
    for rank in range(PEER_TOPK):
        state = body(rank, state)
    return state


def _staircase():
    return [(a, b) for a in range(PEER_TOPK) for b in range(PEER_TOPK) if (a + 1) * (b + 1) <= PEER_TOPK]


GRADE_SUM = float(PEER_TOPK * (PEER_TOPK + 1) // 2)


def _head_top(head, q_scr, k_refs, vals_scrs, grade1_scr, grade2_ref, e1_scr, e2_ref, *, d_half, stable):
    ts = q_scr.shape[0]
    n_slab = ts // LANES
    suspect = jnp.zeros((SUBLANES, LANES), jnp.float32)
    for half, (k_ref, vals_scr) in enumerate(zip(k_refs, vals_scrs)):
        col = (head * 2 + half) * d_half
        st = _mm_nt(k_ref[...], q_scr[:, col:col + d_half])
        if not stable:
            e = jnp.exp(st - jnp.max(st, axis=0, keepdims=True))
            if half == 0:
                e1_scr[head] = e
            else:
                e2_ref[head] = e.astype(e2_ref.dtype)
        chains = [st[:, slab * LANES:(slab + 1) * LANES] for slab in range(n_slab)]
        dests = [(vals_scr, head, slab * LANES) for slab in range(n_slab)]
        coded = _extract_top(chains, dests, stable)
        for slab, s in enumerate(coded):
            grade = _grades(s)
            if not stable:
                total = _col_allreduce(grade, jnp.add)
                suspect = jnp.maximum(suspect, jnp.where(total != GRADE_SUM, 1.0, 0.0))
            lanes = slice(slab * LANES, (slab + 1) * LANES)
            if half == 0:
                grade1_scr[head, :, lanes] = grade
            else:
                grade2_ref[head, :, lanes] = grade.astype(grade2_ref.dtype)
    return suspect


def _chosen_candidates(v1, v2):
    cands = _staircase()
    csum = [v1[a] + v2[b] for a, b in cands]
    ahead = [[] for _ in cands]
    behind = [[] for _ in cands]
    for x, (a, b) in enumerate(cands):
        for y in range(x + 1, len(cands)):
            if a <= cands[y][0] and b <= cands[y][1]:
                continue
            first = jnp.where(csum[x] >= csum[y], 1.0, 0.0)
            ahead[y].append(first)
            behind[x].append(first)
    chosen = []
    for y, (a, b) in enumerate(cands):
        dominators = (a + 1) * (b + 1) - 1
        rank = float(dominators + len(behind[y]))
        if ahead[y]:
            rank = rank + _tree(jnp.add, ahead[y])
        if behind[y]:
            rank = rank - _tree(jnp.add, behind[y])
        chosen.append(rank < float(PEER_TOPK))
    return cands, chosen


def _select_kernel(x_ref, g_ffn_ref, wq_ref, k1_ref, k2_ref,
                   grade2_ref, e2_ref, cut_ref, e1m_ref,
                   q_scr, grade1_scr, e1_scr, v1_scr, v2_scr, *, d_half):
    h2 = _rms(x_ref[...], g_ffn_ref[...])
    q_scr[...] = _mm(h2, wq_ref[...]).astype(q_scr.dtype)
    head_top = functools.partial(_head_top, q_scr=q_scr, k_refs=(k1_ref, k2_ref), vals_scrs=(v1_scr, v2_scr),
                                 grade1_scr=grade1_scr, grade2_ref=grade2_ref, e1_scr=e1_scr, e2_ref=e2_ref,
                                 d_half=d_half)
    redo = [jnp.max(head_top(head, stable=False)) > 0.0 for head in range(PEER_HEADS)]
    for head in range(PEER_HEADS):
        @pl.when(redo[head])
        def _(head=head):
            head_top(head, stable=True)

    v1 = [v1_scr[a] for a in range(PEER_TOPK)]
    v2 = [v2_scr[b] for b in range(PEER_TOPK)]
    cands, chosen = _chosen_candidates(v1, v2)
    ex1 = [jnp.exp(v1[a] - v1[0]) for a in range(PEER_TOPK)]
    ex2 = [jnp.exp(v2[b] - v2[0]) for b in range(PEER_TOPK)]
    z = jnp.zeros(v1[0].shape, jnp.float32)
    limit = [jnp.zeros(v1[0].shape, jnp.float32) for _ in range(PEER_TOPK)]
    for c, (a, b) in enumerate(cands):
        z = z + jnp.where(chosen[c], ex1[a] * ex2[b], 0.0)
        limit[a] = limit[a] + jnp.where(chosen[c], 1.0, 0.0)
    inv_z = 1.0 / z

    for head in range(PEER_HEADS):
        grade1 = grade1_scr[head]
        cut = jnp.full(grade1.shape, float(PEER_TOPK), jnp.float32)
        for a in range(PEER_TOPK):
            cut = jnp.where(grade1 == float(PEER_TOPK - a), float(PEER_TOPK) - limit[a][head:head + 1, :], cut)
        cut_ref[head] = cut
        e1m_ref[head] = e1_scr[head] * inv_z[head:head + 1, :]


def _expert_kernel(x_ref, g_ffn_ref, g_fin_ref, grade2_ref, e2_ref, cut_ref, e1m_ref, u_ref, vt_ref,
                   y_ref, h2_scr, gate_scr, acc_scr):
    eb = pl.program_id(1)
    tokens = x_ref.shape[0]
    experts = u_ref.shape[0]
    keys = grade2_ref.shape[1]

    @pl.when(eb == 0)
    def _():
        h2_scr[...] = _rms(x_ref[...], g_ffn_ref[...]).astype(h2_scr.dtype)
        acc_scr[...] = jnp.zeros_like(acc_scr)

    n_sub = keys // PACKED_ROWS
    for ii in range(experts // keys):
        gates = [jnp.zeros((PACKED_ROWS, tokens), GATE_DTYPE) for _ in range(n_sub)]
        for head in range(PEER_HEADS):
            cut = jnp.broadcast_to(cut_ref[head, ii:ii + 1, :], (PACKED_ROWS, tokens)).astype(GATE_DTYPE)
            e1 = jnp.broadcast_to(e1m_ref[head, ii:ii + 1, :], (PACKED_ROWS, tokens)).astype(GATE_DTYPE)
            for kk in range(n_sub):
                sub = slice(kk * PACKED_ROWS, (kk + 1) * PACKED_ROWS)
                picked = jnp.where(grade2_ref[head, sub, :] > cut, e2_ref[head, sub, :], jnp.zeros_like(cut))
                gates[kk] = gates[kk] + e1 * picked
        for kk in range(n_sub):
            rows = slice(ii * keys + kk * PACKED_ROWS, ii * keys + (kk + 1) * PACKED_ROWS)
            gate_scr[rows, :] = gates[kk]

    act = _gelu_tanh(_mm_nt(u_ref[...], h2_scr[...]), GATE_DTYPE)
    coef = (act * gate_scr[...]).astype(MXU_DTYPE)
    acc_scr[...] += jnp.dot(vt_ref[...], coef, preferred_element_type=jnp.float32)

    @pl.when(eb == pl.num_programs(1) - 1)
    def _():
        x2 = x_ref[...] + acc_scr[...].T
        y_ref[...] = _rms(x2, g_fin_ref[...])


def _pick_tile(n, candidates):
    for c in candidates:
        if n % c == 0:
            return c
    raise ValueError(f"no tile in {candidates} divides {n}")


def _full(shape):
    return pl.BlockSpec(shape, lambda *_: (0,) * len(shape))


def _params(semantics):
    return pltpu.CompilerParams(dimension_semantics=semantics, vmem_limit_bytes=VMEM_LIMIT_BYTES)


def _block_diag(w):
    heads, di, dj = w.shape
    eye = jnp.eye(heads, dtype=w.dtype)
    return jnp.einsum("hij,hk->hikj", w, eye).reshape(heads * di, heads * dj)


def _mixer_weights(g_mix, w_in, g_v, w_s, b_s, conv_w, conv_b, w_a, b_a, w_i, b_i, lam, g_out_a, g_out_b, w_out,
                   chunk_len):
    reps = GMLP_CHUNK // chunk_len
    wmix = jnp.tile(w_s[:, :chunk_len, :chunk_len], (1, reps, reps))
    gd = g_v.shape[-1] // A_GROUPS
    bias = jnp.tile(jnp.repeat(jnp.transpose(b_s[:, :chunk_len]), gd, axis=1), (reps, 1))
    row = lambda p: p.reshape(1, -1)
    return (row(g_mix), w_in.astype(MXU_DTYPE), row(g_v), wmix, bias, conv_w, row(conv_b),
            _block_diag(w_a).astype(MXU_DTYPE), row(b_a), _block_diag(w_i).astype(MXU_DTYPE), row(b_i), row(lam),
            row(g_out_a), row(g_out_b), w_out.astype(MXU_DTYPE))


def _prompt_mixer(x, weights, d_a, d_b):
    batch, seq, d_model = x.shape
    tile = _pick_tile(seq, (512, 256, 128))
    x2 = x.reshape(batch * seq, d_model)
    n_t = seq // tile
    w_specs = [_full(w.shape) for w in weights]
    x1, xb_tail, h_tail = pl.pallas_call(
        functools.partial(_prompt_mixer_kernel, d_a=d_a, d_b=d_b),
        grid=(batch, n_t),
        in_specs=[pl.BlockSpec((tile, d_model), lambda b, t: (b * n_t + t, 0))] + w_specs,
        out_specs=[pl.BlockSpec((tile, d_model), lambda b, t: (b * n_t + t, 0)),
                   pl.BlockSpec((1, SUBLANES, d_b), lambda b, t: (b, 0, 0)),
                   pl.BlockSpec((1, SUBLANES, d_b), lambda b, t: (b, 0, 0))],
        out_shape=[jax.ShapeDtypeStruct((batch * seq, d_model), jnp.float32),
                   jax.ShapeDtypeStruct((batch, SUBLANES, d_b), jnp.float32),
                   jax.ShapeDtypeStruct((batch, SUBLANES, d_b), jnp.float32)],
        scratch_shapes=[pltpu.VMEM((SUBLANES, d_b), jnp.float32), pltpu.VMEM((1, d_b), jnp.float32)],
        compiler_params=_params(("arbitrary", "arbitrary")),
        name="prompt_mixer",
    )(x2, *weights)
    new_conv = xb_tail[:, SUBLANES - (CONV_WIDTH - 1):, :]
    new_lru = h_tail[:, SUBLANES - 1, :]
    return x1, new_conv, new_lru


def _sample_mixer(x, state_conv, state_lru, weights, d_a, d_b):
    batch, seq, d_model = x.shape
    n = batch * seq
    x2 = x.reshape(n, d_model)
    cs = []
    for k in range(1, CONV_WIDTH):
        pad = jnp.zeros((batch, seq - k, d_b), state_conv.dtype)
        cs.append(jnp.concatenate([state_conv[:, CONV_WIDTH - 1 - k:, :], pad], axis=1).reshape(n, d_b))
    cs = jnp.stack(cs)
    h0 = jnp.repeat(state_lru, seq, axis=0)
    args = (x2, cs, h0) + tuple(weights)
    x1, xb, h, v = pl.pallas_call(
        functools.partial(_sample_mixer_kernel, d_a=d_a, d_b=d_b, seq=seq),
        grid=(1,),
        in_specs=[_full(a.shape) for a in args],
        out_specs=[_full((n, d_model)), _full((n, d_b)), _full((n, d_b)), _full((n, d_a))],
        out_shape=[jax.ShapeDtypeStruct((n, d_model), jnp.float32),
                   jax.ShapeDtypeStruct((n, d_b), jnp.float32),
                   jax.ShapeDtypeStruct((n, d_b), jnp.float32),
                   jax.ShapeDtypeStruct((n, d_a), jnp.float32)],
        compiler_params=_params(("arbitrary",)),
        name="sample_mixer",
    )(*args)
    new_conv = xb.reshape(batch, seq, d_b)[:, seq - (CONV_WIDTH - 1):, :]
    new_lru = h.reshape(batch, seq, d_b)[:, seq - 1, :]
    return x1, new_conv, new_lru, v.reshape(batch, seq, d_a)


def _peer_weights(g_ffn, g_fin, w_q, k1, k2, u_tab, v_tab):
    row = lambda p: p.reshape(1, -1)
    vtb = jnp.transpose(v_tab).astype(MXU_DTYPE)
    return (row(g_ffn), row(g_fin), w_q.astype(MXU_DTYPE), k1.astype(MXU_DTYPE), k2.astype(MXU_DTYPE),
            u_tab.astype(MXU_DTYPE), vtb)


def _peer(x1, weights):
    g_ffn, g_fin, wq, k1b, k2b, ub, vtb = weights
    n, d_model = x1.shape
    n_keys, d_half = k1b.shape
    n_experts = ub.shape[0]

    ts = _pick_tile(n, (256, 128))
    table = lambda dt: jax.ShapeDtypeStruct((PEER_HEADS, n_keys, n), dt)
    tile_spec = pl.BlockSpec((PEER_HEADS, n_keys, ts), lambda t: (0, 0, t))
    grade2, e2, cut, e1m = pl.pallas_call(
        functools.partial(_select_kernel, d_half=d_half),
        grid=(n // ts,),
        in_specs=[pl.BlockSpec((ts, d_model), lambda t: (t, 0)), _full(g_ffn.shape), _full(wq.shape),
                  _full(k1b.shape), _full(k2b.shape)],
        out_specs=[tile_spec] * 4,
        out_shape=[table(GATE_DTYPE), table(GATE_DTYPE), table(jnp.float32), table(jnp.float32)],
        scratch_shapes=[pltpu.VMEM((ts, wq.shape[1]), MXU_DTYPE),
                        pltpu.VMEM((PEER_HEADS, n_keys, ts), jnp.float32),
                        pltpu.VMEM((PEER_HEADS, n_keys, ts), jnp.float32),
                        pltpu.VMEM((PEER_TOPK, PEER_HEADS, ts), jnp.float32),
                        pltpu.VMEM((PEER_TOPK, PEER_HEADS, ts), jnp.float32)],
        compiler_params=_params(("arbitrary",)),
        name="peer_select",
    )(x1, g_ffn, wq, k1b, k2b)

    tb = _pick_tile(n, (768, 512, 384, 256, 128))
    i_blk = SUBLANES
    e_blk = i_blk * n_keys
    y = pl.pallas_call(
        _expert_kernel,
        grid=(n // tb, n_experts // e_blk),
        in_specs=[pl.BlockSpec((tb, d_model), lambda t, e: (t, 0)), _full(g_ffn.shape), _full(g_fin.shape),
                  pl.BlockSpec((PEER_HEADS, n_keys, tb), lambda t, e: (0, 0, t)),
                  pl.BlockSpec((PEER_HEADS, n_keys, tb), lambda t, e: (0, 0, t)),
                  pl.BlockSpec((PEER_HEADS, i_blk, tb), lambda t, e: (0, e, t)),
                  pl.BlockSpec((PEER_HEADS, i_blk, tb), lambda t, e: (0, e, t)),
                  pl.BlockSpec((e_blk, d_model), lambda t, e: (e, 0)),
                  pl.BlockSpec((d_model, e_blk), lambda t, e: (0, e))],
        out_specs=pl.BlockSpec((tb, d_model), lambda t, e: (t, 0)),
        out_shape=jax.ShapeDtypeStruct((n, d_model), jnp.float32),
        scratch_shapes=[pltpu.VMEM((tb, d_model), MXU_DTYPE),
                        pltpu.VMEM((e_blk, tb), GATE_DTYPE),
                        pltpu.VMEM((d_model, tb), jnp.float32)],
        compiler_params=_params(("arbitrary", "arbitrary")),
        name="peer_experts",
    )(x1, g_ffn, g_fin, grade2, e2, cut, e1m, ub, vtb)
    return y


def kernel(x_prompt, x_sample, state_conv, state_lru, g_mix, w_in, g_v, w_s, b_s, conv_w, conv_b, w_a, b_a, w_i, b_i, lam, g_out_a, g_out_b, w_out, g_ffn, w_q, k_sub1, k_sub2, u_tab, v_tab, g_final):
    depth = w_in.shape[0]
    assert depth == 1, "the fused final norm assumes a single layer"
    d_a = g_v.shape[-1]
    d_b = conv_b.shape[-1]
    pb, ps, d_model = x_prompt.shape
    sb, ss, _ = x_sample.shape
    l = 0
    layer_w = (g_mix[l], w_in[l], g_v[l], w_s[l], b_s[l], conv_w[l], conv_b[l], w_a[l], b_a[l], w_i[l], b_i[l], lam[l],
               g_out_a[l], g_out_b[l], w_out[l])
    xp1, conv_p, lru_p = _prompt_mixer(x_prompt, _mixer_weights(*layer_w, chunk_len=min(ps, GMLP_CHUNK)), d_a, d_b)
    xs1, conv_s, lru_s, v_s = _sample_mixer(x_sample, state_conv[l], state_lru[l],
                                            _mixer_weights(*layer_w, chunk_len=min(ss, GMLP_CHUNK)), d_a, d_b)
    peer_w = _peer_weights(g_ffn[l], g_final, w_q[l], k_sub1[l], k_sub2[l], u_tab[l], v_tab[l])
    y = _peer(jnp.concatenate([xp1, xs1], axis=0), peer_w)
    n_p = pb * ps
    return (y[:n_p].reshape(pb, ps, d_model), y[n_p:].reshape(sb, ss, d_model),
            conv_p[None], lru_p[None], conv_s[None], lru_s[None], v_s[None])
```

```python
import functools

import jax
import jax.numpy as jnp
from jax import lax
from jax.experimental import pallas as pl
from jax.experimental.pallas import tpu as pltpu

EPS = 1e-6
LRU_C = 8.0
CONV_WIDTH = 4
A_GROUPS = 4
PEER_HEADS = 8
PEER_TOPK = 16
GMLP_CHUNK = 128
SCAN_BLOCK = 64

LANES = 128
SUBLANES = 8
PACKED_ROWS = 16
VMEM_LIMIT_BYTES = 56 * 1024 * 1024

MXU_DTYPE = jnp.bfloat16
GATE_DTYPE = jnp.bfloat16

RANK_BASE = 2.0 ** 100
RANK_STEP = 2.0 ** 80

_NT = (((1,), (1,)), ((), ()))


def _rms(x, g):
    return x * lax.rsqrt(jnp.mean(x * x, axis=-1, keepdims=True) + EPS) * g


def _gelu_tanh(x, dtype):
    c2 = 2.0 * 0.7978845608028654 * 1.4426950408889634
    z = x * (-c2 - (c2 * 0.044715) * (x * x))
    return x.astype(dtype) / (1.0 + jnp.exp2(z.astype(dtype)))


def _mm(a, b):
    return jnp.dot(a.astype(MXU_DTYPE), b.astype(MXU_DTYPE), preferred_element_type=jnp.float32)


def _mm_nt(a, b):
    return lax.dot_general(a.astype(MXU_DTYPE), b.astype(MXU_DTYPE), _NT, preferred_element_type=jnp.float32)


def _seg_scan(a, b, seg):
    rows = lax.broadcasted_iota(jnp.int32, a.shape, 0)
    pos = rows & (seg - 1)
    s = 1
    while s < seg:
        a_sh = pltpu.roll(a, s, 0)
        b_sh = pltpu.roll(b, s, 0)
        valid = pos >= s
        b = jnp.where(valid, a * b_sh + b, b)
        a = jnp.where(valid, a * a_sh, a)
        s *= 2
    return a, b


def _mixer_front(x, g_mix, w_in, g_v, d_a, d_b):
    h = _rms(x, g_mix)
    z = _mm(h, w_in)
    za = jax.nn.gelu(z[:, :2 * d_a])
    u = za[:, :d_a]
    v = _rms(za[:, d_a:], g_v)
    xb = z[:, 2 * d_a:2 * d_a + d_b]
    gate = z[:, 2 * d_a + d_b:]
    return u, v, xb, gate


def _gmlp(u, v, wmix_ref, bias, chunk_len):
    rows, d_a = v.shape
    gd = d_a // A_GROUPS
    r = lax.broadcasted_iota(jnp.int32, (GMLP_CHUNK, GMLP_CHUNK), 0)
    c = lax.broadcasted_iota(jnp.int32, (GMLP_CHUNK, GMLP_CHUNK), 1)
    mask = (r >= c) & ((r // chunk_len) == (c // chunk_len))
    vb = v.astype(MXU_DTYPE)
    blocks = []
    for blk in range(rows // GMLP_CHUNK):
        cols = []
        for g in range(A_GROUPS):
            w = jnp.where(mask, wmix_ref[g], 0.0).astype(MXU_DTYPE)
            vg = vb[blk * GMLP_CHUNK:(blk + 1) * GMLP_CHUNK, g * gd:(g + 1) * gd]
            cols.append(jnp.dot(w, vg, preferred_element_type=jnp.float32))
        blocks.append(jnp.concatenate(cols, axis=1) + bias)
    s = blocks[0] if len(blocks) == 1 else jnp.concatenate(blocks, axis=0)
    return u * s


def _lru_inputs(xc, wa, ba, wi, bi, lam):
    r = jax.nn.sigmoid(_mm(xc, wa) + ba)
    i = jax.nn.sigmoid(_mm(xc, wi) + bi)
    log_sig = jnp.minimum(lam, 0.0) - jnp.log1p(jnp.exp(-jnp.abs(lam)))
    log_a = LRU_C * r * log_sig
    a = jnp.exp(log_a)
    t = jnp.tanh(log_a)
    bx = jnp.sqrt(-2.0 * t / (1.0 - t)) * (i * xc)
    return a, bx


def _mixer_back(x, ya, h, gate, g_out_a, g_out_b, w_out_ref, d_a):
    yb = h * jax.nn.gelu(gate)
    mix = _mm(_rms(ya, g_out_a), w_out_ref[:d_a, :]) + _mm(_rms(yb, g_out_b), w_out_ref[d_a:, :])
    return x + mix


def _prompt_mixer_kernel(x_ref, g_mix_ref, w_in_ref, g_v_ref, wmix_ref, bias_ref, cw_ref, cb_ref, wa_ref, ba_ref,
                         wi_ref, bi_ref, lam_ref, goa_ref, gob_ref, w_out_ref,
                         x1_ref, xb_tail_ref, h_tail_ref, conv_carry, h_carry, *, d_a, d_b):
    t = pl.program_id(1)

    @pl.when(t == 0)
    def _():
        conv_carry[...] = jnp.zeros_like(conv_carry)
        h_carry[...] = jnp.zeros_like(h_carry)

    x = x_ref[...]
    rows = x.shape[0]
    u, v, xb, gate = _mixer_front(x, g_mix_ref[...], w_in_ref[...], g_v_ref[...], d_a, d_b)
    ya = _gmlp(u, v, wmix_ref, bias_ref[...], GMLP_CHUNK)

    cw = cw_ref[...]
    xc = cb_ref[...] + xb * cw[CONV_WIDTH - 1:CONV_WIDTH, :]
    prev = conv_carry[...]
    row8 = lax.broadcasted_iota(jnp.int32, (SUBLANES, d_b), 0)
    for k in range(1, CONV_WIDTH):
        rolled = pltpu.roll(xb, k, 0)
        head = jnp.where(row8 < k, pltpu.roll(prev, k, 0), rolled[:SUBLANES, :])
        shifted = jnp.concatenate([head, rolled[SUBLANES:, :]], axis=0)
        xc = xc + shifted * cw[CONV_WIDTH - 1 - k:CONV_WIDTH - k, :]
    conv_carry[...] = xb[rows - SUBLANES:, :]

    a, bx = _lru_inputs(xc, wa_ref[...], ba_ref[...], wi_ref[...], bi_ref[...], lam_ref[...])
    block = min(SCAN_BLOCK, rows)
    a_cum, b_cum = _seg_scan(a, bx, block)
    state = h_carry[...]
    h_blocks = []
    for r in range(0, rows, block):
        h_blocks.append(a_cum[r:r + block, :] * state + b_cum[r:r + block, :])
        state = h_blocks[-1][block - 1:block, :]
    h = h_blocks[0] if len(h_blocks) == 1 else jnp.concatenate(h_blocks, axis=0)
    h_carry[...] = state

    x1_ref[...] = _mixer_back(x, ya, h, gate, goa_ref[...], gob_ref[...], w_out_ref, d_a)
    xb_tail_ref[0] = xb[rows - SUBLANES:, :]
    h_tail_ref[0] = h[rows - SUBLANES:, :]


def _sample_mixer_kernel(x_ref, cs_ref, h0_ref, g_mix_ref, w_in_ref, g_v_ref, wmix_ref, bias_ref, cw_ref, cb_ref,
                         wa_ref, ba_ref, wi_ref, bi_ref, lam_ref, goa_ref, gob_ref, w_out_ref,
                         x1_ref, xb_ref, h_ref, v_ref, *, d_a, d_b, seq):
    x = x_ref[...]
    u, v, xb, gate = _mixer_front(x, g_mix_ref[...], w_in_ref[...], g_v_ref[...], d_a, d_b)
    ya = _gmlp(u, v, wmix_ref, bias_ref[...], seq)

    cw = cw_ref[...]
    xc = cb_ref[...] + xb * cw[CONV_WIDTH - 1:CONV_WIDTH, :]
    pos = lax.broadcasted_iota(jnp.int32, xb.shape, 0) & (seq - 1)
    for k in range(1, CONV_WIDTH):
        shifted = jnp.where(pos < k, cs_ref[k - 1], pltpu.roll(xb, k, 0))
        xc = xc + shifted * cw[CONV_WIDTH - 1 - k:CONV_WIDTH - k, :]

    a, bx = _lru_inputs(xc, wa_ref[...], ba_ref[...], wi_ref[...], bi_ref[...], lam_ref[...])
    a_cum, b_cum = _seg_scan(a, bx, seq)
    h = a_cum * h0_ref[...] + b_cum

    x1_ref[...] = _mixer_back(x, ya, h, gate, goa_ref[...], gob_ref[...], w_out_ref, d_a)
    xb_ref[...] = xb
    h_ref[...] = h
    v_ref[...] = v


def _tree(op, xs):
    while len(xs) > 1:
        xs = [op(xs[i], xs[i + 1]) if i + 1 < len(xs) else xs[i] for i in range(0, len(xs), 2)]
    return xs[0]


def _col_allreduce(x, op):
    r = _tree(op, [x[k:k + SUBLANES] for k in range(0, x.shape[0], SUBLANES)])
    for shift in (4, 2, 1):
        r = op(r, pltpu.roll(r, shift, 0))
    return r


def _tile_rows(r, rows):
    return jnp.concatenate([r] * (rows // r.shape[0]), axis=0)


def _grade_code(rank):
    return -(RANK_BASE + (float(PEER_TOPK) - rank) * RANK_STEP)


def _grades(coded):
    return jnp.maximum(coded * (-1.0 / RANK_STEP) - RANK_BASE / RANK_STEP, 0.0)


def _extract_top(chains, dests, stable):
    keys = chains[0].shape[0]
    key_idx = lax.broadcasted_iota(jnp.int32, chains[0].shape, 0).astype(jnp.float32)

    def body(rank, state):
        code = _grade_code(jnp.asarray(rank, jnp.float32))
        out = []
        for s, (vals_ref, head, lane0) in zip(state, dests):
            m = _col_allreduce(s, jnp.maximum)
            vals_ref[rank, pl.ds(head, 1), pl.ds(lane0, LANES)] = m[0:1]
            hit = s == _tile_rows(m, keys)
            if stable:
                first = _col_allreduce(jnp.where(hit, key_idx, float(keys)), jnp.minimum)
                hit = key_idx == _tile_rows(first, keys)
            out.append(jnp.where(hit, code, s))
        return tuple(out)

    if stable:
        return lax.fori_loop(0, PEER_TOPK, body, tuple(chains))
    state = tuple(chains)
    for rank in range(PEER_TOPK):
        state = body(rank, state)
    return state


def _staircase():
    return [(a, b) for a in range(PEER_TOPK) for b in range(PEER_TOPK) if (a + 1) * (b + 1) <= PEER_TOPK]


GRADE_SUM = float(PEER_TOPK * (PEER_TOPK + 1) // 2)


def _head_top(head, q_scr, k_refs, vals_scrs, grade1_scr, grade2_ref, e1_scr, e2_ref, *, d_half, stable):
    ts = q_scr.shape[0]
    n_slab = ts // LANES
    suspect = jnp.zeros((SUBLANES, LANES), jnp.float32)
    for half, (k_ref, vals_scr) in enumerate(zip(k_refs, vals_scrs)):
        col = (head * 2 + half) * d_half
        st = _mm_nt(k_ref[...], q_scr[:, col:col + d_half])
        if not stable:
            e = jnp.exp(st - jnp.max(st, axis=0, keepdims=True))
            if half == 0:
                e1_scr[head] = e
            else:
                e2_ref[head] = e.astype(e2_ref.dtype)
        chains = [st[:, slab * LANES:(slab + 1) * LANES] for slab in range(n_slab)]
        dests = [(vals_scr, head, slab * LANES) for slab in range(n_slab)]
        coded = _extract_top(chains, dests, stable)
        for slab, s in enumerate(coded):
            grade = _grades(s)
            if not stable:
                total = _col_allreduce(grade, jnp.add)
                suspect = jnp.maximum(suspect, jnp.where(total != GRADE_SUM, 1.0, 0.0))
            lanes = slice(slab * LANES, (slab + 1) * LANES)
            if half == 0:
                grade1_scr[head, :, lanes] = grade
            else:
                grade2_ref[head, :, lanes] = grade.astype(grade2_ref.dtype)
    return suspect


def _chosen_candidates(v1, v2):
    cands = _staircase()
    csum = [v1[a] + v2[b] for a, b in cands]
    ahead = [[] for _ in cands]
    behind = [[] for _ in cands]
    for x, (a, b) in enumerate(cands):
        for y in range(x + 1, len(cands)):
            if a <= cands[y][0] and b <= cands[y][1]:
                continue
            first = jnp.where(csum[x] >= csum[y], 1.0, 0.0)
            ahead[y].append(first)
            behind[x].append(first)
    chosen = []
    for y, (a, b) in enumerate(cands):
        dominators = (a + 1) * (b + 1) - 1
        rank = float(dominators + len(behind[y]))
        if ahead[y]:
            rank = rank + _tree(jnp.add, ahead[y])
        if behind[y]:
            rank = rank - _tree(jnp.add, behind[y])
        chosen.append(rank < float(PEER_TOPK))
    return cands, chosen


def _select_kernel(x_ref, g_ffn_ref, wq_ref, k1_ref, k2_ref,
                   grade2_ref, e2_ref, cut_ref, e1m_ref,
                   q_scr, grade1_scr, e1_scr, v1_scr, v2_scr, *, d_half):
    h2 = _rms(x_ref[...], g_ffn_ref[...])
    q_scr[...] = _mm(h2, wq_ref[...]).astype(q_scr.dtype)
    head_top = functools.partial(_head_top, q_scr=q_scr, k_refs=(k1_ref, k2_ref), vals_scrs=(v1_scr, v2_scr),
                                 grade1_scr=grade1_scr, grade2_ref=grade2_ref, e1_scr=e1_scr, e2_ref=e2_ref,
                                 d_half=d_half)
    redo = [jnp.max(head_top(head, stable=False)) > 0.0 for head in range(PEER_HEADS)]
    for head in range(PEER_HEADS):
        @pl.when(redo[head])
        def _(head=head):
            head_top(head, stable=True)

    v1 = [v1_scr[a] for a in range(PEER_TOPK)]
    v2 = [v2_scr[b] for b in range(PEER_TOPK)]
    cands, chosen = _chosen_candidates(v1, v2)
    ex1 = [jnp.exp(v1[a] - v1[0]) for a in range(PEER_TOPK)]
    ex2 = [jnp.exp(v2[b] - v2[0]) for b in range(PEER_TOPK)]
    z = jnp.zeros(v1[0].shape, jnp.float32)
    limit = [jnp.zeros(v1[0].shape, jnp.float32) for _ in range(PEER_TOPK)]
    for c, (a, b) in enumerate(cands):
        z = z + jnp.where(chosen[c], ex1[a] * ex2[b], 0.0)
        limit[a] = limit[a] + jnp.where(chosen[c], 1.0, 0.0)
    inv_z = 1.0 / z

    cuts = [(float(PEER_TOPK) - limit[a]).astype(GATE_DTYPE) for a in range(PEER_TOPK)]
    for head in range(PEER_HEADS):
        grade1 = grade1_scr[head].astype(GATE_DTYPE)
        cut = jnp.full(grade1.shape, float(PEER_TOPK), GATE_DTYPE)
        for a in range(PEER_TOPK):
            cut = jnp.where(grade1 == float(PEER_TOPK - a), cuts[a][head:head + 1, :], cut)
        cut_ref[head] = cut.astype(cut_ref.dtype)
        e1m_ref[head] = e1_scr[head] * inv_z[head:head + 1, :]


def _expert_kernel(x_ref, g_ffn_ref, g_fin_ref, grade2_ref, e2_ref, cut_ref, e1m_ref, u_ref, vt_ref,
                   y_ref, h2_scr, gate_scr, acc_scr):
    eb = pl.program_id(1)
    tokens = x_ref.shape[0]
    experts = u_ref.shape[0]
    keys = grade2_ref.shape[1]

    @pl.when(eb == 0)
    def _():
        h2_scr[...] = _rms(x_ref[...], g_ffn_ref[...]).astype(h2_scr.dtype)
        acc_scr[...] = jnp.zeros_like(acc_scr)

    n_sub = keys // PACKED_ROWS
    for ii in range(experts // keys):
        gates = [jnp.zeros((PACKED_ROWS, tokens), GATE_DTYPE) for _ in range(n_sub)]
        for head in range(PEER_HEADS):
            cut = jnp.broadcast_to(cut_ref[head, ii:ii + 1, :], (PACKED_ROWS, tokens)).astype(GATE_DTYPE)
            e1 = jnp.broadcast_to(e1m_ref[head, ii:ii + 1, :], (PACKED_ROWS, tokens)).astype(GATE_DTYPE)
            for kk in range(n_sub):
                sub = slice(kk * PACKED_ROWS, (kk + 1) * PACKED_ROWS)
                picked = jnp.where(grade2_ref[head, sub, :] > cut, e2_ref[head, sub, :], jnp.zeros_like(cut))
                gates[kk] = gates[kk] + e1 * picked
        for kk in range(n_sub):
            rows = slice(ii * keys + kk * PACKED_ROWS, ii * keys + (kk + 1) * PACKED_ROWS)
            gate_scr[rows, :] = gates[kk]

    act = _gelu_tanh(_mm_nt(u_ref[...], h2_scr[...]), GATE_DTYPE)
    coef = (act * gate_scr[...]).astype(MXU_DTYPE)
    acc_scr[...] += jnp.dot(vt_ref[...], coef, preferred_element_type=jnp.float32)

    @pl.when(eb == pl.num_programs(1) - 1)
    def _():
        x2 = x_ref[...] + acc_scr[...].T
        y_ref[...] = _rms(x2, g_fin_ref[...])


def _pick_tile(n, candidates):
    for c in candidates:
        if n % c == 0:
            return c
    raise ValueError(f"no tile in {candidates} divides {n}")


def _full(shape):
    return pl.BlockSpec(shape, lambda *_: (0,) * len(shape))


def _params(semantics):
    return pltpu.CompilerParams(dimension_semantics=semantics, vmem_limit_bytes=VMEM_LIMIT_BYTES)


def _block_diag(w):
    heads, di, dj = w.shape
    eye = jnp.eye(heads, dtype=w.dtype)
    return jnp.einsum("hij,hk->hikj", w, eye).reshape(heads * di, heads * dj)


def _mixer_weights(g_mix, w_in, g_v, w_s, b_s, conv_w, conv_b, w_a, b_a, w_i, b_i, lam, g_out_a, g_out_b, w_out,
                   chunk_len):
    reps = GMLP_CHUNK // chunk_len
    wmix = jnp.tile(w_s[:, :chunk_len, :chunk_len], (1, reps, reps))
    gd = g_v.shape[-1] // A_GROUPS
    bias = jnp.tile(jnp.repeat(jnp.transpose(b_s[:, :chunk_len]), gd, axis=1), (reps, 1))
    row = lambda p: p.reshape(1, -1)
    return (row(g_mix), w_in.astype(MXU_DTYPE), row(g_v), wmix, bias, conv_w, row(conv_b),
            _block_diag(w_a).astype(MXU_DTYPE), row(b_a), _block_diag(w_i).astype(MXU_DTYPE), row(b_i), row(lam),
            row(g_out_a), row(g_out_b), w_out.astype(MXU_DTYPE))


def _prompt_mixer(x, weights, d_a, d_b):
    batch, seq, d_model = x.shape
    tile = _pick_tile(seq, (512, 256, 128))
    x2 = x.reshape(batch * seq, d_model)
    n_t = seq // tile
    w_specs = [_full(w.shape) for w in weights]
    x1, xb_tail, h_tail = pl.pallas_call(
        functools.partial(_prompt_mixer_kernel, d_a=d_a, d_b=d_b),
        grid=(batch, n_t),
        in_specs=[pl.BlockSpec((tile, d_model), lambda b, t: (b * n_t + t, 0))] + w_specs,
        out_specs=[pl.BlockSpec((tile, d_model), lambda b, t: (b * n_t + t, 0)),
                   pl.BlockSpec((1, SUBLANES, d_b), lambda b, t: (b, 0, 0)),
                   pl.BlockSpec((1, SUBLANES, d_b), lambda b, t: (b, 0, 0))],
        out_shape=[jax.ShapeDtypeStruct((batch * seq, d_model), jnp.float32),
                   jax.ShapeDtypeStruct((batch, SUBLANES, d_b), jnp.float32),
                   jax.ShapeDtypeStruct((batch, SUBLANES, d_b), jnp.float32)],
        scratch_shapes=[pltpu.VMEM((SUBLANES, d_b), jnp.float32), pltpu.VMEM((1, d_b), jnp.float32)],
        compiler_params=_params(("arbitrary", "arbitrary")),
        name="prompt_mixer",
    )(x2, *weights)
    new_conv = xb_tail[:, SUBLANES - (CONV_WIDTH - 1):, :]
    new_lru = h_tail[:, SUBLANES - 1, :]
    return x1, new_conv, new_lru


def _sample_mixer(x, state_conv, state_lru, weights, d_a, d_b):
    batch, seq, d_model = x.shape
    n = batch * seq
    x2 = x.reshape(n, d_model)
    cs = []
    for k in range(1, CONV_WIDTH):
        pad = jnp.zeros((batch, seq - k, d_b), state_conv.dtype)
        cs.append(jnp.concatenate([state_conv[:, CONV_WIDTH - 1 - k:, :], pad], axis=1).reshape(n, d_b))
    cs = jnp.stack(cs)
    h0 = jnp.repeat(state_lru, seq, axis=0)
    args = (x2, cs, h0) + tuple(weights)
    x1, xb, h, v = pl.pallas_call(
        functools.partial(_sample_mixer_kernel, d_a=d_a, d_b=d_b, seq=seq),
        grid=(1,),
        in_specs=[_full(a.shape) for a in args],
        out_specs=[_full((n, d_model)), _full((n, d_b)), _full((n, d_b)), _full((n, d_a))],
        out_shape=[jax.ShapeDtypeStruct((n, d_model), jnp.float32),
                   jax.ShapeDtypeStruct((n, d_b), jnp.float32),
                   jax.ShapeDtypeStruct((n, d_b), jnp.float32),
                   jax.ShapeDtypeStruct((n, d_a), jnp.float32)],
        compiler_params=_params(("arbitrary",)),
        name="sample_mixer",
    )(*args)
    new_conv = xb.reshape(batch, seq, d_b)[:, seq - (CONV_WIDTH - 1):, :]
    new_lru = h.reshape(batch, seq, d_b)[:, seq - 1, :]
    return x1, new_conv, new_lru, v.reshape(batch, seq, d_a)


def _peer_weights(g_ffn, g_fin, w_q, k1, k2, u_tab, v_tab):
    row = lambda p: p.reshape(1, -1)
    vtb = jnp.transpose(v_tab).astype(MXU_DTYPE)
    return (row(g_ffn), row(g_fin), w_q.astype(MXU_DTYPE), k1.astype(MXU_DTYPE), k2.astype(MXU_DTYPE),
            u_tab.astype(MXU_DTYPE), vtb)


def _peer(x1, weights):
    g_ffn, g_fin, wq, k1b, k2b, ub, vtb = weights
    n, d_model = x1.shape
    n_keys, d_half = k1b.shape
    n_experts = ub.shape[0]

    ts = _pick_tile(n, (256, 128))
    table = lambda dt: jax.ShapeDtypeStruct((PEER_HEADS, n_keys, n), dt)
    tile_spec = pl.BlockSpec((PEER_HEADS, n_keys, ts), lambda t: (0, 0, t))
    grade2, e2, cut, e1m = pl.pallas_call(
        functools.partial(_select_kernel, d_half=d_half),
        grid=(n // ts,),
        in_specs=[pl.BlockSpec((ts, d_model), lambda t: (t, 0)), _full(g_ffn.shape), _full(wq.shape),
                  _full(k1b.shape), _full(k2b.shape)],
        out_specs=[tile_spec] * 4,
        out_shape=[table(GATE_DTYPE), table(GATE_DTYPE), table(jnp.float32), table(jnp.float32)],
        scratch_shapes=[pltpu.VMEM((ts, wq.shape[1]), MXU_DTYPE),
                        pltpu.VMEM((PEER_HEADS, n_keys, ts), jnp.float32),
                        pltpu.VMEM((PEER_HEADS, n_keys, ts), jnp.float32),
                        pltpu.VMEM((PEER_TOPK, PEER_HEADS, ts), jnp.float32),
                        pltpu.VMEM((PEER_TOPK, PEER_HEADS, ts), jnp.float32)],
        compiler_params=_params(("arbitrary",)),
        name="peer_select",
    )(x1, g_ffn, wq, k1b, k2b)

    tb = _pick_tile(n, (768, 512, 384, 256, 128))
    i_blk = SUBLANES
    e_blk = i_blk * n_keys
    y = pl.pallas_call(
        _expert_kernel,
        grid=(n // tb, n_experts // e_blk),
        in_specs=[pl.BlockSpec((tb, d_model), lambda t, e: (t, 0)), _full(g_ffn.shape), _full(g_fin.shape),
                  pl.BlockSpec((PEER_HEADS, n_keys, tb), lambda t, e: (0, 0, t)),
                  pl.BlockSpec((PEER_HEADS, n_keys, tb), lambda t, e: (0, 0, t)),
                  pl.BlockSpec((PEER_HEADS, i_blk, tb), lambda t, e: (0, e, t)),
                  pl.BlockSpec((PEER_HEADS, i_blk, tb), lambda t, e: (0, e, t)),
                  pl.BlockSpec((e_blk, d_model), lambda t, e: (e, 0)),
                  pl.BlockSpec((d_model, e_blk), lambda t, e: (0, e))],
        out_specs=pl.BlockSpec((tb, d_model), lambda t, e: (t, 0)),
        out_shape=jax.ShapeDtypeStruct((n, d_model), jnp.float32),
        scratch_shapes=[pltpu.VMEM((tb, d_model), MXU_DTYPE),
                        pltpu.VMEM((e_blk, tb), GATE_DTYPE),
                        pltpu.VMEM((d_model, tb), jnp.float32)],
        compiler_params=_params(("arbitrary", "arbitrary")),
        name="peer_experts",
    )(x1, g_ffn, g_fin, grade2, e2, cut, e1m, ub, vtb)
    return y


def kernel(x_prompt, x_sample, state_conv, state_lru, g_mix, w_in, g_v, w_s, b_s, conv_w, conv_b, w_a, b_a, w_i, b_i, lam, g_out_a, g_out_b, w_out, g_ffn, w_q, k_sub1, k_sub2, u_tab, v_tab, g_final):
    depth = w_in.shape[0]
    assert depth == 1, "the fused final norm assumes a single layer"
    d_a = g_v.shape[-1]
    d_b = conv_b.shape[-1]
    pb, ps, d_model = x_prompt.shape
    sb, ss, _ = x_sample.shape
    l = 0
    layer_w = (g_mix[l], w_in[l], g_v[l], w_s[l], b_s[l], conv_w[l], conv_b[l], w_a[l], b_a[l], w_i[l], b_i[l], lam[l],
               g_out_a[l], g_out_b[l], w_out[l])
    xp1, conv_p, lru_p = _prompt_mixer(x_prompt, _mixer_weights(*layer_w, chunk_len=min(ps, GMLP_CHUNK)), d_a, d_b)
    xs1, conv_s, lru_s, v_s = _sample_mixer(x_sample, state_conv[l], state_lru[l],
                                            _mixer_weights(*layer_w, chunk_len=min(ss, GMLP_CHUNK)), d_a, d_b)
    peer_w = _peer_weights(g_ffn[l], g_final, w_q[l], k_sub1[l], k_sub2[l], u_tab[l], v_tab[l])
    y = _peer(jnp.concatenate([xp1, xs1], axis=0), peer_w)
    n_p = pb * ps
    return (y[:n_p].reshape(pb, ps, d_model), y[n_p:].reshape(sb, ss, d_model),
            conv_p[None], lru_p[None], conv_s[None], lru_s[None], v_s[None])
```

```python
import functools

import jax
import jax.numpy as jnp
from jax import lax
from jax.experimental import pallas as pl
from jax.experimental.pallas import tpu as pltpu

EPS = 1e-6
LRU_C = 8.0
CONV_WIDTH = 4
A_GROUPS = 4
PEER_HEADS = 8
PEER_TOPK = 16
GMLP_CHUNK = 128
SCAN_BLOCK = 32

LANES = 128
SUBLANES = 8
PACKED_ROWS = 16
VMEM_LIMIT_BYTES = 56 * 1024 * 1024

MXU_DTYPE = jnp.bfloat16
GATE_DTYPE = jnp.bfloat16

RANK_BASE = 2.0 ** 100
RANK_STEP = 2.0 ** 80

_NT = (((1,), (1,)), ((), ()))


def _rms(x, g):
    return x * lax.rsqrt(jnp.mean(x * x, axis=-1, keepdims=True) + EPS) * g


def _gelu_tanh(x, dtype):
    c2 = 2.0 * 0.7978845608028654 * 1.4426950408889634
    z = x * (-c2 - (c2 * 0.044715) * (x * x))
    return x.astype(dtype) / (1.0 + jnp.exp2(z.astype(dtype)))


def _mm(a, b):
    return jnp.dot(a.astype(MXU_DTYPE), b.astype(MXU_DTYPE), preferred_element_type=jnp.float32)


def _mm_nt(a, b):
    return lax.dot_general(a.astype(MXU_DTYPE), b.astype(MXU_DTYPE), _NT, preferred_element_type=jnp.float32)


def _seg_scan(a, b, seg):
    rows = lax.broadcasted_iota(jnp.int32, a.shape, 0)
    pos = rows & (seg - 1)
    s = 1
    while s < seg:
        a_sh = pltpu.roll(a, s, 0)
        b_sh = pltpu.roll(b, s, 0)
        valid = pos >= s
        b = jnp.where(valid, a * b_sh + b, b)
        a = jnp.where(valid, a * a_sh, a)
        s *= 2
    return a, b


def _mixer_front(x, g_mix, w_in, g_v, d_a, d_b):
    h = _rms(x, g_mix)
    z = _mm(h, w_in)
    za = jax.nn.gelu(z[:, :2 * d_a])
    u = za[:, :d_a]
    v = _rms(za[:, d_a:], g_v)
    xb = z[:, 2 * d_a:2 * d_a + d_b]
    gate = z[:, 2 * d_a + d_b:]
    return u, v, xb, gate


def _gmlp(u, v, wmix_ref, bias, chunk_len):
    rows, d_a = v.shape
    gd = d_a // A_GROUPS
    r = lax.broadcasted_iota(jnp.int32, (GMLP_CHUNK, GMLP_CHUNK), 0)
    c = lax.broadcasted_iota(jnp.int32, (GMLP_CHUNK, GMLP_CHUNK), 1)
    mask = (r >= c) & ((r // chunk_len) == (c // chunk_len))
    vb = v.astype(MXU_DTYPE)
    blocks = []
    for blk in range(rows // GMLP_CHUNK):
        cols = []
        for g in range(A_GROUPS):
            w = jnp.where(mask, wmix_ref[g], 0.0).astype(MXU_DTYPE)
            vg = vb[blk * GMLP_CHUNK:(blk + 1) * GMLP_CHUNK, g * gd:(g + 1) * gd]
            cols.append(jnp.dot(w, vg, preferred_element_type=jnp.float32))
        blocks.append(jnp.concatenate(cols, axis=1) + bias)
    s = blocks[0] if len(blocks) == 1 else jnp.concatenate(blocks, axis=0)
    return u * s


def _lru_inputs(xc, wa, ba, wi, bi, lam):
    r = jax.nn.sigmoid(_mm(xc, wa) + ba)
    i = jax.nn.sigmoid(_mm(xc, wi) + bi)
    log_sig = jnp.minimum(lam, 0.0) - jnp.log1p(jnp.exp(-jnp.abs(lam)))
    log_a = LRU_C * r * log_sig
    a = jnp.exp(log_a)
    t = jnp.tanh(log_a)
    bx = jnp.sqrt(-2.0 * t / (1.0 - t)) * (i * xc)
    return a, bx


def _mixer_back(x, ya, h, gate, g_out_a, g_out_b, w_out_ref, d_a):
    yb = h * jax.nn.gelu(gate)
    mix = _mm(_rms(ya, g_out_a), w_out_ref[:d_a, :]) + _mm(_rms(yb, g_out_b), w_out_ref[d_a:, :])
    return x + mix


def _prompt_mixer_kernel(x_ref, g_mix_ref, w_in_ref, g_v_ref, wmix_ref, bias_ref, cw_ref, cb_ref, wa_ref, ba_ref,
                         wi_ref, bi_ref, lam_ref, goa_ref, gob_ref, w_out_ref,
                         x1_ref, xb_tail_ref, h_tail_ref, conv_carry, h_carry, *, d_a, d_b):
    t = pl.program_id(1)

    @pl.when(t == 0)
    def _():
        conv_carry[...] = jnp.zeros_like(conv_carry)
        h_carry[...] = jnp.zeros_like(h_carry)

    x = x_ref[...]
    rows = x.shape[0]
    u, v, xb, gate = _mixer_front(x, g_mix_ref[...], w_in_ref[...], g_v_ref[...], d_a, d_b)
    ya = _gmlp(u, v, wmix_ref, bias_ref[...], GMLP_CHUNK)

    cw = cw_ref[...]
    xc = cb_ref[...] + xb * cw[CONV_WIDTH - 1:CONV_WIDTH, :]
    prev = conv_carry[...]
    row8 = lax.broadcasted_iota(jnp.int32, (SUBLANES, d_b), 0)
    for k in range(1, CONV_WIDTH):
        rolled = pltpu.roll(xb, k, 0)
        head = jnp.where(row8 < k, pltpu.roll(prev, k, 0), rolled[:SUBLANES, :])
        shifted = jnp.concatenate([head, rolled[SUBLANES:, :]], axis=0)
        xc = xc + shifted * cw[CONV_WIDTH - 1 - k:CONV_WIDTH - k, :]
    conv_carry[...] = xb[rows - SUBLANES:, :]

    a, bx = _lru_inputs(xc, wa_ref[...], ba_ref[...], wi_ref[...], bi_ref[...], lam_ref[...])
    block = min(SCAN_BLOCK, rows)
    a_cum, b_cum = _seg_scan(a, bx, block)
    state = h_carry[...]
    h_blocks = []
    for r in range(0, rows, block):
        h_blocks.append(a_cum[r:r + block, :] * state + b_cum[r:r + block, :])
        state = h_blocks[-1][block - 1:block, :]
    h = h_blocks[0] if len(h_blocks) == 1 else jnp.concatenate(h_blocks, axis=0)
    h_carry[...] = state

    x1_ref[...] = _mixer_back(x, ya, h, gate, goa_ref[...], gob_ref[...], w_out_ref, d_a)
    xb_tail_ref[0] = xb[rows - SUBLANES:, :]
    h_tail_ref[0] = h[rows - SUBLANES:, :]


def _sample_mixer_kernel(x_ref, cs_ref, h0_ref, g_mix_ref, w_in_ref, g_v_ref, wmix_ref, bias_ref, cw_ref, cb_ref,
                         wa_ref, ba_ref, wi_ref, bi_ref, lam_ref, goa_ref, gob_ref, w_out_ref,
                         x1_ref, xb_ref, h_ref, v_ref, *, d_a, d_b, seq):
    x = x_ref[...]
    u, v, xb, gate = _mixer_front(x, g_mix_ref[...], w_in_ref[...], g_v_ref[...], d_a, d_b)
    ya = _gmlp(u, v, wmix_ref, bias_ref[...], seq)

    cw = cw_ref[...]
    xc = cb_ref[...] + xb * cw[CONV_WIDTH - 1:CONV_WIDTH, :]
    pos = lax.broadcasted_iota(jnp.int32, xb.shape, 0) & (seq - 1)
    for k in range(1, CONV_WIDTH):
        shifted = jnp.where(pos < k, cs_ref[k - 1], pltpu.roll(xb, k, 0))
        xc = xc + shifted * cw[CONV_WIDTH - 1 - k:CONV_WIDTH - k, :]

    a, bx = _lru_inputs(xc, wa_ref[...], ba_ref[...], wi_ref[...], bi_ref[...], lam_ref[...])
    a_cum, b_cum = _seg_scan(a, bx, seq)
    h = a_cum * h0_ref[...] + b_cum

    x1_ref[...] = _mixer_back(x, ya, h, gate, goa_ref[...], gob_ref[...], w_out_ref, d_a)
    xb_ref[...] = xb
    h_ref[...] = h
    v_ref[...] = v


def _tree(op, xs):
    while len(xs) > 1:
        xs = [op(xs[i], xs[i + 1]) if i + 1 < len(xs) else xs[i] for i in range(0, len(xs), 2)]
    return xs[0]


def _col_allreduce(x, op):
    r = _tree(op, [x[k:k + SUBLANES] for k in range(0, x.shape[0], SUBLANES)])
    for shift in (4, 2, 1):
        r = op(r, pltpu.roll(r, shift, 0))
    return r


def _tile_rows(r, rows):
    return jnp.concatenate([r] * (rows // r.shape[0]), axis=0)


def _grade_code(rank):
    return -(RANK_BASE + (float(PEER_TOPK) - rank) * RANK_STEP)


def _grades(coded):
    return jnp.maximum(coded * (-1.0 / RANK_STEP) - RANK_BASE / RANK_STEP, 0.0)


def _extract_top(chains, dests, stable):
    keys = chains[0].shape[0]
    key_idx = lax.broadcasted_iota(jnp.int32, chains[0].shape, 0).astype(jnp.float32)

    def body(rank, state):
        code = _grade_code(jnp.asarray(rank, jnp.float32))
        out = []
        for s, (vals_ref, head, lane0) in zip(state, dests):
            m = _col_allreduce(s, jnp.maximum)
            vals_ref[rank, pl.ds(head, 1), pl.ds(lane0, LANES)] = m[0:1]
            hit = s == _tile_rows(m, keys)
            if stable:
                first = _col_allreduce(jnp.where(hit, key_idx, float(keys)), jnp.minimum)
                hit = key_idx == _tile_rows(first, keys)
            out.append(jnp.where(hit, code, s))
        return tuple(out)

    if stable:
        return lax.fori_loop(0, PEER_TOPK, body, tuple(chains))
    state = tuple(chains)
    for rank in range(PEER_TOPK):
        state = body(rank, state)
    return state


def _staircase():
    return [(a, b) for a in range(PEER_TOPK) for b in range(PEER_TOPK) if (a + 1) * (b + 1) <= PEER_TOPK]


GRADE_SUM = float(PEER_TOPK * (PEER_TOPK + 1) // 2)


def _head_top(head, q_scr, k_refs, vals_scrs, grade1_scr, grade2_ref, e1_scr, e2_ref, *, d_half, stable):
    ts = q_scr.shape[0]
    n_slab = ts // LANES
    suspect = jnp.zeros((SUBLANES, LANES), jnp.float32)
    for half, (k_ref, vals_scr) in enumerate(zip(k_refs, vals_scrs)):
        col = (head * 2 + half) * d_half
        st = _mm_nt(k_ref[...], q_scr[:, col:col + d_half])
        if not stable:
            e = jnp.exp(st - jnp.max(st, axis=0, keepdims=True))
            if half == 0:
                e1_scr[head] = e
            else:
                e2_ref[head] = e.astype(e2_ref.dtype)
        chains = [st[:, slab * LANES:(slab + 1) * LANES] for slab in range(n_slab)]
        dests = [(vals_scr, head, slab * LANES) for slab in range(n_slab)]
        coded = _extract_top(chains, dests, stable)
        for slab, s in enumerate(coded):
            grade = _grades(s)
            if not stable:
                total = _col_allreduce(grade, jnp.add)
                suspect = jnp.maximum(suspect, jnp.where(total != GRADE_SUM, 1.0, 0.0))
            lanes = slice(slab * LANES, (slab + 1) * LANES)
            if half == 0:
                grade1_scr[head, :, lanes] = grade
            else:
                grade2_ref[head, :, lanes] = grade.astype(grade2_ref.dtype)
    return suspect


def _chosen_candidates(v1, v2):
    cands = _staircase()
    csum = [v1[a] + v2[b] for a, b in cands]
    ahead = [[] for _ in cands]
    behind = [[] for _ in cands]
    for x, (a, b) in enumerate(cands):
        for y in range(x + 1, len(cands)):
            if a <= cands[y][0] and b <= cands[y][1]:
                continue
            first = jnp.where(csum[x] >= csum[y], 1.0, 0.0)
            ahead[y].append(first)
            behind[x].append(first)
    chosen = []
    for y, (a, b) in enumerate(cands):
        dominators = (a + 1) * (b + 1) - 1
        rank = float(dominators + len(behind[y]))
        if ahead[y]:
            rank = rank + _tree(jnp.add, ahead[y])
        if behind[y]:
            rank = rank - _tree(jnp.add, behind[y])
        chosen.append(rank < float(PEER_TOPK))
    return cands, chosen


def _select_kernel(x_ref, g_ffn_ref, wq_ref, k1_ref, k2_ref,
                   grade2_ref, e2_ref, cut_ref, e1m_ref,
                   q_scr, grade1_scr, e1_scr, v1_scr, v2_scr, *, d_half):
    h2 = _rms(x_ref[...], g_ffn_ref[...])
    q_scr[...] = _mm(h2, wq_ref[...]).astype(q_scr.dtype)
    head_top = functools.partial(_head_top, q_scr=q_scr, k_refs=(k1_ref, k2_ref), vals_scrs=(v1_scr, v2_scr),
                                 grade1_scr=grade1_scr, grade2_ref=grade2_ref, e1_scr=e1_scr, e2_ref=e2_ref,
                                 d_half=d_half)
    redo = [jnp.max(head_top(head, stable=False)) > 0.0 for head in range(PEER_HEADS)]
    for head in range(PEER_HEADS):
        @pl.when(redo[head])
        def _(head=head):
            head_top(head, stable=True)

    v1 = [v1_scr[a] for a in range(PEER_TOPK)]
    v2 = [v2_scr[b] for b in range(PEER_TOPK)]
    cands, chosen = _chosen_candidates(v1, v2)
    ex1 = [jnp.exp(v1[a] - v1[0]) for a in range(PEER_TOPK)]
    ex2 = [jnp.exp(v2[b] - v2[0]) for b in range(PEER_TOPK)]
    z = jnp.zeros(v1[0].shape, jnp.float32)
    limit = [jnp.zeros(v1[0].shape, jnp.float32) for _ in range(PEER_TOPK)]
    for c, (a, b) in enumerate(cands):
        z = z + jnp.where(chosen[c], ex1[a] * ex2[b], 0.0)
        limit[a] = limit[a] + jnp.where(chosen[c], 1.0, 0.0)
    inv_z = 1.0 / z

    cuts = [(float(PEER_TOPK) - limit[a]).astype(GATE_DTYPE) for a in range(PEER_TOPK)]
    for head in range(PEER_HEADS):
        grade1 = grade1_scr[head].astype(GATE_DTYPE)
        cut = jnp.full(grade1.shape, float(PEER_TOPK), GATE_DTYPE)
        for a in range(PEER_TOPK):
            cut = jnp.where(grade1 == float(PEER_TOPK - a), cuts[a][head:head + 1, :], cut)
        cut_ref[head] = cut.astype(cut_ref.dtype)
        e1m_ref[head] = e1_scr[head] * inv_z[head:head + 1, :]


def _expert_kernel(x_ref, g_ffn_ref, g_fin_ref, grade2_ref, e2_ref, cut_ref, e1m_ref, u_ref, vt_ref,
                   y_ref, h2_scr, gate_scr, acc_scr):
    eb = pl.program_id(1)
    tokens = x_ref.shape[0]
    experts = u_ref.shape[0]
    keys = grade2_ref.shape[1]

    @pl.when(eb == 0)
    def _():
        h2_scr[...] = _rms(x_ref[...], g_ffn_ref[...]).astype(h2_scr.dtype)
        acc_scr[...] = jnp.zeros_like(acc_scr)

    n_sub = keys // PACKED_ROWS
    for ii in range(experts // keys):
        gates = [jnp.zeros((PACKED_ROWS, tokens), GATE_DTYPE) for _ in range(n_sub)]
        for head in range(PEER_HEADS):
            cut = jnp.broadcast_to(cut_ref[head, ii:ii + 1, :], (PACKED_ROWS, tokens)).astype(GATE_DTYPE)
            e1 = jnp.broadcast_to(e1m_ref[head, ii:ii + 1, :], (PACKED_ROWS, tokens)).astype(GATE_DTYPE)
            for kk in range(n_sub):
                sub = slice(kk * PACKED_ROWS, (kk + 1) * PACKED_ROWS)
                picked = jnp.where(grade2_ref[head, sub, :] > cut, e2_ref[head, sub, :], jnp.zeros_like(cut))
                gates[kk] = gates[kk] + e1 * picked
        for kk in range(n_sub):
            rows = slice(ii * keys + kk * PACKED_ROWS, ii * keys + (kk + 1) * PACKED_ROWS)
            gate_scr[rows, :] = gates[kk]

    act = _gelu_tanh(_mm_nt(u_ref[...], h2_scr[...]), GATE_DTYPE)
    coef = (act * gate_scr[...]).astype(MXU_DTYPE)
    acc_scr[...] += jnp.dot(vt_ref[...], coef, preferred_element_type=jnp.float32)

    @pl.when(eb == pl.num_programs(1) - 1)
    def _():
        x2 = x_ref[...] + acc_scr[...].T
        y_ref[...] = _rms(x2, g_fin_ref[...])


def _pick_tile(n, candidates):
    for c in candidates:
        if n % c == 0:
            return c
    raise ValueError(f"no tile in {candidates} divides {n}")


def _full(shape):
    return pl.BlockSpec(shape, lambda *_: (0,) * len(shape))


def _params(semantics):
    return pltpu.CompilerParams(dimension_semantics=semantics, vmem_limit_bytes=VMEM_LIMIT_BYTES)


def _block_diag(w):
    heads, di, dj = w.shape
    eye = jnp.eye(heads, dtype=w.dtype)
    return jnp.einsum("hij,hk->hikj", w, eye).reshape(heads * di, heads * dj)


def _mixer_weights(g_mix, w_in, g_v, w_s, b_s, conv_w, conv_b, w_a, b_a, w_i, b_i, lam, g_out_a, g_out_b, w_out,
                   chunk_len):
    reps = GMLP_CHUNK // chunk_len
    wmix = jnp.tile(w_s[:, :chunk_len, :chunk_len], (1, reps, reps))
    gd = g_v.shape[-1] // A_GROUPS
    bias = jnp.tile(jnp.repeat(jnp.transpose(b_s[:, :chunk_len]), gd, axis=1), (reps, 1))
    row = lambda p: p.reshape(1, -1)
    return (row(g_mix), w_in.astype(MXU_DTYPE), row(g_v), wmix, bias, conv_w, row(conv_b),
            _block_diag(w_a).astype(MXU_DTYPE), row(b_a), _block_diag(w_i).astype(MXU_DTYPE), row(b_i), row(lam),
            row(g_out_a), row(g_out_b), w_out.astype(MXU_DTYPE))


def _prompt_mixer(x, weights, d_a, d_b):
    batch, seq, d_model = x.shape
    tile = _pick_tile(seq, (512, 256, 128))
    x2 = x.reshape(batch * seq, d_model)
    n_t = seq // tile
    w_specs = [_full(w.shape) for w in weights]
    x1, xb_tail, h_tail = pl.pallas_call(
        functools.partial(_prompt_mixer_kernel, d_a=d_a, d_b=d_b),
        grid=(batch, n_t),
        in_specs=[pl.BlockSpec((tile, d_model), lambda b, t: (b * n_t + t, 0))] + w_specs,
        out_specs=[pl.BlockSpec((tile, d_model), lambda b, t: (b * n_t + t, 0)),
                   pl.BlockSpec((1, SUBLANES, d_b), lambda b, t: (b, 0, 0)),
                   pl.BlockSpec((1, SUBLANES, d_b), lambda b, t: (b, 0, 0))],
        out_shape=[jax.ShapeDtypeStruct((batch * seq, d_model), jnp.float32),
                   jax.ShapeDtypeStruct((batch, SUBLANES, d_b), jnp.float32),
                   jax.ShapeDtypeStruct((batch, SUBLANES, d_b), jnp.float32)],
        scratch_shapes=[pltpu.VMEM((SUBLANES, d_b), jnp.float32), pltpu.VMEM((1, d_b), jnp.float32)],
        compiler_params=_params(("arbitrary", "arbitrary")),
        name="prompt_mixer",
    )(x2, *weights)
    new_conv = xb_tail[:, SUBLANES - (CONV_WIDTH - 1):, :]
    new_lru = h_tail[:, SUBLANES - 1, :]
    return x1, new_conv, new_lru


def _sample_mixer(x, state_conv, state_lru, weights, d_a, d_b):
    batch, seq, d_model = x.shape
    n = batch * seq
    x2 = x.reshape(n, d_model)
    cs = []
    for k in range(1, CONV_WIDTH):
        pad = jnp.zeros((batch, seq - k, d_b), state_conv.dtype)
        cs.append(jnp.concatenate([state_conv[:, CONV_WIDTH - 1 - k:, :], pad], axis=1).reshape(n, d_b))
    cs = jnp.stack(cs)
    h0 = jnp.repeat(state_lru, seq, axis=0)
    args = (x2, cs, h0) + tuple(weights)
    x1, xb, h, v = pl.pallas_call(
        functools.partial(_sample_mixer_kernel, d_a=d_a, d_b=d_b, seq=seq),
        grid=(1,),
        in_specs=[_full(a.shape) for a in args],
        out_specs=[_full((n, d_model)), _full((n, d_b)), _full((n, d_b)), _full((n, d_a))],
        out_shape=[jax.ShapeDtypeStruct((n, d_model), jnp.float32),
                   jax.ShapeDtypeStruct((n, d_b), jnp.float32),
                   jax.ShapeDtypeStruct((n, d_b), jnp.float32),
                   jax.ShapeDtypeStruct((n, d_a), jnp.float32)],
        compiler_params=_params(("arbitrary",)),
        name="sample_mixer",
    )(*args)
    new_conv = xb.reshape(batch, seq, d_b)[:, seq - (CONV_WIDTH - 1):, :]
    new_lru = h.reshape(batch, seq, d_b)[:, seq - 1, :]
    return x1, new_conv, new_lru, v.reshape(batch, seq, d_a)


def _peer_weights(g_ffn, g_fin, w_q, k1, k2, u_tab, v_tab):
    row = lambda p: p.reshape(1, -1)
    vtb = jnp.transpose(v_tab).astype(MXU_DTYPE)
    return (row(g_ffn), row(g_fin), w_q.astype(MXU_DTYPE), k1.astype(MXU_DTYPE), k2.astype(MXU_DTYPE),
            u_tab.astype(MXU_DTYPE), vtb)


def _peer(x1, weights):
    g_ffn, g_fin, wq, k1b, k2b, ub, vtb = weights
    n, d_model = x1.shape
    n_keys, d_half = k1b.shape
    n_experts = ub.shape[0]

    ts = _pick_tile(n, (512, 256, 128))
    table = lambda dt: jax.ShapeDtypeStruct((PEER_HEADS, n_keys, n), dt)
    tile_spec = pl.BlockSpec((PEER_HEADS, n_keys, ts), lambda t: (0, 0, t))
    grade2, e2, cut, e1m = pl.pallas_call(
        functools.partial(_select_kernel, d_half=d_half),
        grid=(n // ts,),
        in_specs=[pl.BlockSpec((ts, d_model), lambda t: (t, 0)), _full(g_ffn.shape), _full(wq.shape),
                  _full(k1b.shape), _full(k2b.shape)],
        out_specs=[tile_spec] * 4,
        out_shape=[table(GATE_DTYPE), table(GATE_DTYPE), table(jnp.float32), table(jnp.float32)],
        scratch_shapes=[pltpu.VMEM((ts, wq.shape[1]), MXU_DTYPE),
                        pltpu.VMEM((PEER_HEADS, n_keys, ts), jnp.float32),
                        pltpu.VMEM((PEER_HEADS, n_keys, ts), jnp.float32),
                        pltpu.VMEM((PEER_TOPK, PEER_HEADS, ts), jnp.float32),
                        pltpu.VMEM((PEER_TOPK, PEER_HEADS, ts), jnp.float32)],
        compiler_params=_params(("arbitrary",)),
        name="peer_select",
    )(x1, g_ffn, wq, k1b, k2b)

    tb = _pick_tile(n, (768, 512, 384, 256, 128))
    i_blk = 2 * SUBLANES
    e_blk = i_blk * n_keys
    y = pl.pallas_call(
        _expert_kernel,
        grid=(n // tb, n_experts // e_blk),
        in_specs=[pl.BlockSpec((tb, d_model), lambda t, e: (t, 0)), _full(g_ffn.shape), _full(g_fin.shape),
                  pl.BlockSpec((PEER_HEADS, n_keys, tb), lambda t, e: (0, 0, t)),
                  pl.BlockSpec((PEER_HEADS, n_keys, tb), lambda t, e: (0, 0, t)),
                  pl.BlockSpec((PEER_HEADS, i_blk, tb), lambda t, e: (0, e, t)),
                  pl.BlockSpec((PEER_HEADS, i_blk, tb), lambda t, e: (0, e, t)),
                  pl.BlockSpec((e_blk, d_model), lambda t, e: (e, 0)),
                  pl.BlockSpec((d_model, e_blk), lambda t, e: (0, e))],
        out_specs=pl.BlockSpec((tb, d_model), lambda t, e: (t, 0)),
        out_shape=jax.ShapeDtypeStruct((n, d_model), jnp.float32),
        scratch_shapes=[pltpu.VMEM((tb, d_model), MXU_DTYPE),
                        pltpu.VMEM((e_blk, tb), GATE_DTYPE),
                        pltpu.VMEM((d_model, tb), jnp.float32)],
        compiler_params=_params(("arbitrary", "arbitrary")),
        name="peer_experts",
    )(x1, g_ffn, g_fin, grade2, e2, cut, e1m, ub, vtb)
    return y


def kernel(x_prompt, x_sample, state_conv, state_lru, g_mix, w_in, g_v, w_s, b_s, conv_w, conv_b, w_a, b_a, w_i, b_i, lam, g_out_a, g_out_b, w_out, g_ffn, w_q, k_sub1, k_sub2, u_tab, v_tab, g_final):
    depth = w_in.shape[0]
    assert depth == 1, "the fused final norm assumes a single layer"
    d_a = g_v.shape[-1]
    d_b = conv_b.shape[-1]
    pb, ps, d_model = x_prompt.shape
    sb, ss, _ = x_sample.shape
    l = 0
    layer_w = (g_mix[l], w_in[l], g_v[l], w_s[l], b_s[l], conv_w[l], conv_b[l], w_a[l], b_a[l], w_i[l], b_i[l], lam[l],
               g_out_a[l], g_out_b[l], w_out[l])
    xp1, conv_p, lru_p = _prompt_mixer(x_prompt, _mixer_weights(*layer_w, chunk_len=min(ps, GMLP_CHUNK)), d_a, d_b)
    xs1, conv_s, lru_s, v_s = _sample_mixer(x_sample, state_conv[l], state_lru[l],
                                            _mixer_weights(*layer_w, chunk_len=min(ss, GMLP_CHUNK)), d_a, d_b)
    peer_w = _peer_weights(g_ffn[l], g_final, w_q[l], k_sub1[l], k_sub2[l], u_tab[l], v_tab[l])
    y = _peer(jnp.concatenate([xp1, xs1], axis=0), peer_w)
    n_p = pb * ps
    return (y[:n_p].reshape(pb, ps, d_model), y[n_p:].reshape(sb, ss, d_model),
            conv_p[None], lru_p[None], conv_s[None], lru_s[None], v_s[None])
```
